```python
import math
import jax, jax.numpy as jnp
from jax import lax
import numpy as np

D_MODEL = 1024
BATCH = 8
SEQ = 4096
DEPTH = 2

CHUNK = 64
Q_BLOCK = 128
DA_HEADS = 8
DA_HEAD_DIM = 64
DA_QK_WIDTH = DA_HEADS * 2 * DA_HEAD_DIM
DA_V_WIDTH = DA_HEADS * 2 * DA_HEAD_DIM
S5_GROUP = 16
S5_WIDTH = D_MODEL
S5_GROUPS = S5_WIDTH // S5_GROUP
S5_STATE = 64
CONV_WIDTH = D_MODEL
CONV_KERNEL = 31
FFN_HIDDEN = 2816
FFN_KERNEL = 3
N_BRANCHES = 3
REL_BUCKETS = 32
REL_MAX_DIST = 128
EPS = 1e-6
OFF_Q = 0
OFF_K = OFF_Q + DA_QK_WIDTH
OFF_V = OFF_K + DA_QK_WIDTH
OFF_U = OFF_V + DA_V_WIDTH
OFF_C = OFF_U + S5_WIDTH
OFF_G = OFF_C + 2 * CONV_WIDTH
IN_WIDTH = OFF_G + N_BRANCHES * D_MODEL

kernel_name = 'hybrid_diffattn_s5_conformer_encoder'


def _rmsnorm(x, g):
    x32 = x.astype(jnp.float32)
    y = x32 * lax.rsqrt(jnp.mean(x32 * x32, axis=-1, keepdims=True) + EPS)
    return (y * g.astype(jnp.float32)).astype(x.dtype)


def _layernorm(x, g, b):
    x32 = x.astype(jnp.float32)
    xc = x32 - jnp.mean(x32, axis=-1, keepdims=True)
    y = xc * lax.rsqrt(jnp.mean(xc * xc, axis=-1, keepdims=True) + EPS)
    return (y * g.astype(jnp.float32) + b.astype(jnp.float32)).astype(x.dtype)


def _causal_dwconv(x, w):
    k_len, ch = w.shape
    return lax.conv_general_dilated(
        x, w[:, None, :].astype(x.dtype), window_strides=(1,), padding=[(k_len - 1, 0)],
        dimension_numbers=('NWC', 'WIO', 'NWC'), feature_group_count=ch)


def _t5_bucket(rel):
    nb = REL_BUCKETS // 2
    n = -rel
    ret = jnp.where(n < 0, nb, 0)
    n = jnp.abs(n)
    max_exact = nb // 2
    nf = jnp.maximum(n, 1).astype(jnp.float32)
    large = max_exact + (jnp.log(nf / max_exact) / math.log(REL_MAX_DIST / max_exact)
                         * (nb - max_exact)).astype(jnp.int32)
    large = jnp.minimum(large, nb - 1)
    return ret + jnp.where(n < max_exact, n, large)


def _diff_attention(q, k, v, bias_table, lam):
    b_, s_ = q.shape[:2]
    n_blocks = s_ // Q_BLOCK
    kpos = jnp.arange(s_)
    kchunk = kpos // CHUNK
    scale = DA_HEAD_DIM ** -0.5
    qb = jnp.moveaxis(q.reshape(b_, n_blocks, Q_BLOCK, DA_HEADS, 2, DA_HEAD_DIM), 1, 0)

    def block(args):
        q_blk, idx = args
        qpos = idx * Q_BLOCK + jnp.arange(Q_BLOCK)
        s = jnp.einsum('bqhcd,bkhcd->bhcqk', q_blk, k,
                       preferred_element_type=jnp.float32) * scale
        bias = jnp.moveaxis(bias_table[_t5_bucket(kpos[None, :] - qpos[:, None])], -1, 0)
        mask = kchunk[None, :] <= (qpos // CHUNK)[:, None]
        s = jnp.where(mask, s + bias.astype(jnp.float32)[None, :, None], -1e30)
        p = jax.nn.softmax(s, axis=-1)
        attn = p[:, :, 0] - lam * p[:, :, 1]
        return jnp.einsum('bhqk,bkhe->bqhe', attn.astype(v.dtype), v)

    out = lax.map(block, (qb, jnp.arange(n_blocks)))
    return jnp.moveaxis(out, 0, 1).reshape(b_, s_, DA_HEADS, 2 * DA_HEAD_DIM)


def _s5(u, lam_re, lam_im, log_step, b_re, b_im, c_re, c_im, d):
    f32 = jnp.float32
    b_, s_ = u.shape[:2]
    n_chunks = s_ // CHUNK
    step = jnp.exp(log_step.astype(f32))[:, None]
    lr, li = lam_re.astype(f32), lam_im.astype(f32)
    mag = jnp.exp(lr * step)
    ab_re, ab_im = mag * jnp.cos(li * step), mag * jnp.sin(li * step)
    den = lr * lr + li * li
    nr, ni = ab_re - 1.0, ab_im
    f_re = (nr * lr + ni * li) / den
    f_im = (ni * lr - nr * li) / den
    br, bi = b_re.astype(f32), b_im.astype(f32)
    bb_re = f_re[..., None] * br - f_im[..., None] * bi
    bb_im = f_re[..., None] * bi + f_im[..., None] * br
    cr, ci, d32 = c_re.astype(f32), c_im.astype(f32), d.astype(f32)
    ug = jnp.moveaxis(u.astype(f32).reshape(b_, n_chunks, CHUNK, S5_GROUPS, S5_GROUP), 1, 0)
    a_re = jnp.broadcast_to(ab_re, (b_, CHUNK, S5_GROUPS, S5_STATE))
    a_im = jnp.broadcast_to(ab_im, (b_, CHUNK, S5_GROUPS, S5_STATE))

    def combine(e1, e2):
        a1r, a1i, b1r, b1i = e1
        a2r, a2i, b2r, b2i = e2
        return (a2r * a1r - a2i * a1i, a2r * a1i + a2i * a1r,
                a2r * b1r - a2i * b1i + b2r, a2r * b1i + a2i * b1r + b2i)

    def chunk_step(carry, uc):
        xr0, xi0 = carry
        bur = jnp.einsum('blgh,gph->blgp', uc, bb_re)
        bui = jnp.einsum('blgh,gph->blgp', uc, bb_im)
        acr, aci, sr, si = lax.associative_scan(combine, (a_re, a_im, bur, bui), axis=1)
        xr = sr + acr * xr0[:, None] - aci * xi0[:, None]
        xi = si + acr * xi0[:, None] + aci * xr0[:, None]
        y = (jnp.einsum('blgp,ghp->blgh', xr, cr) - jnp.einsum('blgp,ghp->blgh', xi, ci)
             + d32 * uc)
        return (xr[:, -1], xi[:, -1]), y

    init = (jnp.zeros((b_, S5_GROUPS, S5_STATE), f32), jnp.zeros((b_, S5_GROUPS, S5_STATE), f32))
    _, y = lax.scan(chunk_step, init, ug)
    return jnp.moveaxis(y, 0, 1).reshape(b_, s_, S5_WIDTH).astype(u.dtype)


def setup_inputs(seed: int = 0) -> dict:
    key = jax.random.key(seed)
    ks = jax.random.split(key, 32)
    f32 = jnp.float32

    def nrm(k, shape, scale):
        return jax.random.normal(k, shape, f32) * scale

    def gain(k, shape):
        return 1.0 + 0.02 * jax.random.normal(k, shape, f32)

    L = DEPTH
    return {
        'x': jax.random.normal(ks[0], (BATCH, SEQ, D_MODEL), f32),
        'norm_mix': gain(ks[1], (L, D_MODEL)),
        'w_in': nrm(ks[2], (L, D_MODEL, IN_WIDTH), D_MODEL ** -0.5),
        'qk_gain_q': gain(ks[3], (L, DA_HEAD_DIM)),
        'qk_gain_k': gain(ks[4], (L, DA_HEAD_DIM)),
        'lambda_q1': nrm(ks[5], (L, DA_HEAD_DIM), 0.1),
        'lambda_k1': nrm(ks[6], (L, DA_HEAD_DIM), 0.1),
        'lambda_q2': nrm(ks[7], (L, DA_HEAD_DIM), 0.1),
        'lambda_k2': nrm(ks[8], (L, DA_HEAD_DIM), 0.1),
        'diff_subln': gain(ks[9], (L, 2 * DA_HEAD_DIM)),
        'rel_bias': nrm(ks[10], (REL_BUCKETS, DA_HEADS), 0.5),
        'w_attn_out': nrm(ks[11], (L, DA_V_WIDTH, D_MODEL), DA_V_WIDTH ** -0.5),
        's5_lambda_re': -0.5 + 0.01 * jax.random.normal(ks[12], (L, S5_GROUPS, S5_STATE), f32),
        's5_lambda_im': (jnp.pi * jnp.arange(S5_STATE, dtype=f32)
                         + 0.01 * jax.random.normal(ks[13], (L, S5_GROUPS, S5_STATE), f32)),
        's5_log_step': jax.random.uniform(ks[14], (L, S5_GROUPS), f32,
                                          minval=math.log(0.001), maxval=math.log(0.1)),
        's5_b_re': nrm(ks[15], (L, S5_GROUPS, S5_STATE, S5_GROUP), (2 * S5_GROUP) ** -0.5),
        's5_b_im': nrm(ks[16], (L, S5_GROUPS, S5_STATE, S5_GROUP), (2 * S5_GROUP) ** -0.5),
        's5_c_re': nrm(ks[17], (L, S5_GROUPS, S5_GROUP, S5_STATE), (2 * S5_STATE) ** -0.5),
        's5_c_im': nrm(ks[18], (L, S5_GROUPS, S5_GROUP, S5_STATE), (2 * S5_STATE) ** -0.5),
        's5_d': nrm(ks[19], (L, S5_GROUPS, S5_GROUP), 1.0),
        's5_glu_w1': nrm(ks[20], (L, S5_WIDTH, D_MODEL), S5_WIDTH ** -0.5),
        's5_glu_w2': nrm(ks[21], (L, S5_WIDTH, D_MODEL), S5_WIDTH ** -0.5),
        'conv_dw_w': nrm(ks[22], (L, CONV_KERNEL, CONV_WIDTH), CONV_KERNEL ** -0.5),
        'conv_dw_b': nrm(ks[23], (L, CONV_WIDTH), 0.02),
        'conv_ln_g': gain(ks[24], (L, CONV_WIDTH)),
        'conv_ln_b': nrm(ks[25], (L, CONV_WIDTH), 0.02),
        'conv_w_out': nrm(ks[26], (L, CONV_WIDTH, D_MODEL), CONV_WIDTH ** -0.5),
        'w_out': nrm(ks[27], (L, D_MODEL, D_MODEL), D_MODEL ** -0.5),
        'norm_ffn': gain(ks[28], (L, D_MODEL)),
        'ffn_w_up': nrm(ks[29], (L, D_MODEL, 2 * FFN_HIDDEN), D_MODEL ** -0.5),
        'ffn_dw_w': nrm(ks[30], (L, FFN_KERNEL, 2 * FFN_HIDDEN), FFN_KERNEL ** -0.5),
        'ffn_w_down': nrm(ks[31], (L, FFN_HIDDEN, D_MODEL), FFN_HIDDEN ** -0.5),
    }


def reference(x, norm_mix, w_in, qk_gain_q, qk_gain_k, lambda_q1, lambda_k1, lambda_q2, lambda_k2,
              diff_subln, rel_bias, w_attn_out, s5_lambda_re, s5_lambda_im, s5_log_step,
              s5_b_re, s5_b_im, s5_c_re, s5_c_im, s5_d, s5_glu_w1, s5_glu_w2,
              conv_dw_w, conv_dw_b, conv_ln_g, conv_ln_b, conv_w_out, w_out,
              norm_ffn, ffn_w_up, ffn_dw_w, ffn_w_down):
    f32 = jnp.float32
    b_, s_ = x.shape[:2]
    for l in range(DEPTH):
        h = _rmsnorm(x, norm_mix[l])
        z = h @ w_in[l]
        q = z[..., OFF_Q:OFF_K].reshape(b_, s_, DA_HEADS, 2, DA_HEAD_DIM)
        k = z[..., OFF_K:OFF_V].reshape(b_, s_, DA_HEADS, 2, DA_HEAD_DIM)
        v = z[..., OFF_V:OFF_U].reshape(b_, s_, DA_HEADS, 2 * DA_HEAD_DIM)
        u = z[..., OFF_U:OFF_C]
        c_in = z[..., OFF_C:OFF_G]
        gates = jax.nn.sigmoid(z[..., OFF_G:].astype(f32)).reshape(b_, s_, N_BRANCHES, D_MODEL)

        q = _rmsnorm(q, qk_gain_q[l])
        k = _rmsnorm(k, qk_gain_k[l])
        lam_init = 0.8 - 0.6 * math.exp(-0.3 * l)
        lam = (jnp.exp(jnp.sum(lambda_q1[l].astype(f32) * lambda_k1[l].astype(f32)))
               - jnp.exp(jnp.sum(lambda_q2[l].astype(f32) * lambda_k2[l].astype(f32))) + lam_init)
        o = _diff_attention(q, k, v, rel_bias, lam)
        o = _rmsnorm(o, diff_subln[l]) * (1.0 - lam_init)
        br_a = o.reshape(b_, s_, DA_V_WIDTH) @ w_attn_out[l]

        y = jax.nn.gelu(_s5(u, s5_lambda_re[l], s5_lambda_im[l], s5_log_step[l], s5_b_re[l],
                            s5_b_im[l], s5_c_re[l], s5_c_im[l], s5_d[l]))
        br_b = (y @ s5_glu_w1[l]) * jax.nn.sigmoid(y @ s5_glu_w2[l])

        c = c_in[..., :CONV_WIDTH] * jax.nn.sigmoid(c_in[..., CONV_WIDTH:])
        c = _causal_dwconv(c, conv_dw_w[l]) + conv_dw_b[l]
        c = jax.nn.silu(_layernorm(c, conv_ln_g[l], conv_ln_b[l]))
        br_c = c @ conv_w_out[l]

        mix = (gates[:, :, 0] * br_a.astype(f32) + gates[:, :, 1] * br_b.astype(f32)
               + gates[:, :, 2] * br_c.astype(f32)).astype(x.dtype)
        x = x + mix @ w_out[l]

        h2 = _rmsnorm(x, norm_ffn[l])
        up = _causal_dwconv(h2 @ ffn_w_up[l], ffn_dw_w[l])
        x = x + (jax.nn.gelu(up[..., FFN_HIDDEN:]) * up[..., :FFN_HIDDEN]) @ ffn_w_down[l]
    return x
```

```python
import functools
import math

import jax
import jax.numpy as jnp
from jax import lax
from jax.experimental import pallas as pl
from jax.experimental.pallas import tpu as pltpu

F32 = jnp.float32
BF16 = jnp.bfloat16

D_MODEL = 1024
DEPTH = 2
CHUNK = 64
HEADS = 8
HEAD_DIM = 64
HEAD_WIDTH = 2 * HEAD_DIM
S5_GROUP = 16
S5_GROUPS = 64
S5_STATE = 64
CONV_KERNEL = 31
FFN_HIDDEN = 2816
FFN_KERNEL = 3
REL_BUCKETS = 32
REL_MAX_DIST = 128
EPS = 1e-6
IN_TILES = 9
NEG_INF = -1e30

LANES = 128
SUBLANES = 8
MXU_DIM = 256
VMEM_LIMIT = 56 * 1024 * 1024

ROW_TILE = 512
ATTN_TILE = 256
S5_STEPS = 32
S5_SLABS = D_MODEL // MXU_DIM
S5_SLAB_STATE = (MXU_DIM // S5_GROUP) * S5_STATE
CONV_HALO = 32
CONV_TILE = 256
CONV_ROWS = 64
FFN_HALO = 16
FFN_TILE = 256
FFN_TILES = FFN_HIDDEN // FFN_TILE


def _params(semantics):
    return pltpu.CompilerParams(dimension_semantics=semantics, vmem_limit_bytes=VMEM_LIMIT)


def _rms_rows(x, g):
    return x * lax.rsqrt(jnp.mean(x * x, axis=-1, keepdims=True) + EPS) * g


def _head_norm(z, gain, scale):
    lane = lax.broadcasted_iota(jnp.int32, (1, HEAD_WIDTH), 1)
    first = lane < HEAD_DIM
    outs = []
    for h in range(HEADS):
        slab = z[:, h * HEAD_WIDTH:(h + 1) * HEAD_WIDTH]
        sq = slab * slab
        s0 = jnp.sum(jnp.where(first, sq, 0.0), axis=-1, keepdims=True)
        s1 = jnp.sum(jnp.where(first, 0.0, sq), axis=-1, keepdims=True)
        r = lax.rsqrt(jnp.where(first, s0, s1) * (1.0 / HEAD_DIM) + EPS)
        outs.append((slab * r * gain * scale).astype(BF16))
    return jnp.concatenate(outs, axis=-1)


def _in_proj_kernel(x_ref, g_ref, w_ref, gq_ref, gk_ref,
                    q_ref, k_ref, v_ref, u_ref, c_ref, gate_ref, h_scr, a_scr):
    j = pl.program_id(1)

    @pl.when(j == 0)
    def _():
        h_scr[...] = _rms_rows(x_ref[...], g_ref[...]).astype(BF16)

    z = jnp.dot(h_scr[...], w_ref[...], preferred_element_type=F32)

    @pl.when(j == 0)
    def _():
        q_ref[...] = _head_norm(z, gq_ref[...], HEAD_DIM ** -0.5)

    @pl.when(j == 1)
    def _():
        k_ref[...] = _head_norm(z, gk_ref[...], 1.0)

    @pl.when(j == 2)
    def _():
        v_ref[...] = z.astype(BF16)

    @pl.when(j == 3)
    def _():
        u_ref[...] = z

    @pl.when(j == 4)
    def _():
        a_scr[...] = z

    @pl.when(j == 5)
    def _():
        c_ref[...] = (a_scr[...] * jax.nn.sigmoid(z)).astype(BF16)

    @pl.when(j >= 6)
    def _():
        gate_ref[...] = jax.nn.sigmoid(z).astype(BF16)


def _in_proj(x2d, g, w, gq, gk, batch, seq):
    n = x2d.shape[0]
    tm = ROW_TILE
    nsb = seq // tm
    row = lambda i, j: (i, 0)
    return pl.pallas_call(
        _in_proj_kernel,
        grid=(n // tm, IN_TILES),
        in_specs=[
            pl.BlockSpec((tm, D_MODEL), row),
            pl.BlockSpec((1, D_MODEL), lambda i, j: (0, 0)),
            pl.BlockSpec((D_MODEL, D_MODEL), lambda i, j: (0, j)),
            pl.BlockSpec((1, HEAD_WIDTH), lambda i, j: (0, 0)),
            pl.BlockSpec((1, HEAD_WIDTH), lambda i, j: (0, 0)),
        ],
        out_specs=[
            pl.BlockSpec((tm, D_MODEL), row),
            pl.BlockSpec((tm, D_MODEL), row),
            pl.BlockSpec((tm, D_MODEL), row),
            pl.BlockSpec((tm, D_MODEL), lambda i, j: (i % nsb, i // nsb)),
            pl.BlockSpec((tm, D_MODEL), row),
            pl.BlockSpec((tm, D_MODEL), lambda i, j: (i, jnp.maximum(j - 6, 0))),
        ],
        out_shape=[
            jax.ShapeDtypeStruct((n, D_MODEL), BF16),
            jax.ShapeDtypeStruct((n, D_MODEL), BF16),
            jax.ShapeDtypeStruct((n, D_MODEL), BF16),
            jax.ShapeDtypeStruct((seq, batch * D_MODEL), F32),
            jax.ShapeDtypeStruct((n, D_MODEL), BF16),
            jax.ShapeDtypeStruct((n, 3 * D_MODEL), BF16),
        ],
        scratch_shapes=[pltpu.VMEM((tm, D_MODEL), BF16), pltpu.VMEM((tm, D_MODEL), F32)],
        compiler_params=_params(("parallel", "arbitrary")),
        name="in_proj",
    )(x2d, g, w, gq, gk)


def _t5_bucket_ids(rel):
    nb = REL_BUCKETS // 2
    n = -rel
    ret = jnp.where(n < 0, nb, 0)
    n = jnp.abs(n)
    max_exact = nb // 2
    nf = jnp.maximum(n, 1).astype(F32)
    large = max_exact + (jnp.log(nf / max_exact) / math.log(REL_MAX_DIST / max_exact)
                         * (nb - max_exact)).astype(jnp.int32)
    large = jnp.minimum(large, nb - 1)
    return ret + jnp.where(n < max_exact, n, large)


def _bias_kernel(far_ref, tab_ref, ids_ref, out_ref):
    h = pl.program_id(0)
    t = ATTN_TILE
    ids = ids_ref[...]
    acc = jnp.zeros(ids.shape, F32)
    for b in range(REL_BUCKETS):
        acc = jnp.where(ids == b, tab_ref[b, h], acc)
    acc = acc - tab_ref[far_ref[0], h]
    qi = lax.broadcasted_iota(jnp.int32, (t, t), 0)
    kj = lax.broadcasted_iota(jnp.int32, (t, t), 1)
    visible = (kj // CHUNK) <= (qi // CHUNK)
    out_ref[0] = jnp.where(visible, acc[0], NEG_INF)
    out_ref[1] = acc[1]


def _bias_tiles(rel_bias):
    t = ATTN_TILE
    qi = jnp.arange(t, dtype=jnp.int32)[:, None]
    kj = jnp.arange(t, dtype=jnp.int32)[None, :]
    ids = jnp.stack([_t5_bucket_ids(kj - qi), _t5_bucket_ids(kj - qi - t)])
    far = _t5_bucket_ids(jnp.full((1,), -(t + 1), jnp.int32))
    return pl.pallas_call(
        _bias_kernel,
        grid=(HEADS,),
        in_specs=[
            pl.BlockSpec(memory_space=pltpu.SMEM),
            pl.BlockSpec(memory_space=pltpu.SMEM),
            pl.BlockSpec((2, t, t), lambda h: (0, 0, 0)),
        ],
        out_specs=pl.BlockSpec((None, 2, t, t), lambda h: (h, 0, 0, 0)),
        out_shape=jax.ShapeDtypeStruct((HEADS, 2, t, t), F32),
        compiler_params=_params(("parallel",)),
        name="bias_tiles",
    )(far, rel_bias, ids)


def _attn_kernel(lamv_ref, subln_ref, bias_ref, q_ref, k_ref, v_ref, o_ref,
                 m_scr, l_scr, acc_scr, *, lam_init):
    t = ATTN_TILE
    i = pl.program_id(2)
    q = q_ref[...]
    lane = lax.broadcasted_iota(jnp.int32, (1, HEAD_WIDTH), 1)
    first = lane < HEAD_DIM
    zero = jnp.zeros_like(q)
    qs = jnp.concatenate([jnp.where(first, q, zero), jnp.where(first, zero, q)], axis=0)

    m_scr[...] = jnp.full(m_scr.shape, NEG_INF, F32)
    l_scr[...] = jnp.zeros(l_scr.shape, F32)
    acc_scr[...] = jnp.zeros(acc_scr.shape, F32)

    def scores(j):
        kb = k_ref[pl.ds(pl.multiple_of(j * t, t), t), :]
        return lax.dot_general(qs, kb, (((1,), (1,)), ((), ())), preferred_element_type=F32)

    def update(s, j):
        vb = v_ref[pl.ds(pl.multiple_of(j * t, t), t), :]
        m_prev = m_scr[...]
        m_new = jnp.maximum(m_prev, jnp.max(s, axis=-1, keepdims=True))
        alpha = jnp.exp(m_prev - m_new)
        p = jnp.exp(s - m_new)
        l_scr[...] = alpha * l_scr[...] + jnp.sum(p, axis=-1, keepdims=True)
        acc_scr[...] = alpha * acc_scr[...] + jnp.dot(p.astype(BF16), vb, preferred_element_type=F32)
        m_scr[...] = m_new

    def far_body(j, carry):
        update(scores(j), j)
        return carry

    lax.fori_loop(0, i - 1, far_body, 0)

    @pl.when(i >= 1)
    def _():
        b1 = bias_ref[1]
        update(scores(i - 1) + jnp.concatenate([b1, b1], axis=0), i - 1)

    b0 = bias_ref[0]
    update(scores(i) + jnp.concatenate([b0, b0], axis=0), i)

    lv = lamv_ref[...]
    lam = (jnp.exp(jnp.sum(lv[0:1] * lv[1:2], axis=-1, keepdims=True))
           - jnp.exp(jnp.sum(lv[2:3] * lv[3:4], axis=-1, keepdims=True)) + lam_init)
    o = acc_scr[0:t] / l_scr[0:t] - lam * (acc_scr[t:2 * t] / l_scr[t:2 * t])
    o_ref[...] = (_rms_rows(o, subln_ref[...]) * (1.0 - lam_init)).astype(BF16)


def _attention(q, k, v, bias, lamv, subln, lam_init):
    b, s, _ = q.shape
    t = ATTN_TILE
    return pl.pallas_call(
        functools.partial(_attn_kernel, lam_init=lam_init),
        grid=(b, HEADS, s // t),
        in_specs=[
            pl.BlockSpec((4, HEAD_DIM), lambda bi, h, i: (0, 0)),
            pl.BlockSpec((1, HEAD_WIDTH), lambda bi, h, i: (0, 0)),
            pl.BlockSpec((None, 2, t, t), lambda bi, h, i: (h, 0, 0, 0)),
            pl.BlockSpec((None, t, HEAD_WIDTH), lambda bi, h, i: (bi, i, h)),
            pl.BlockSpec((None, s, HEAD_WIDTH), lambda bi, h, i: (bi, 0, h)),
            pl.BlockSpec((None, s, HEAD_WIDTH), lambda bi, h, i: (bi, 0, h)),
        ],
        out_specs=pl.BlockSpec((None, t, HEAD_WIDTH), lambda bi, h, i: (bi, i, h)),
        out_shape=jax.ShapeDtypeStruct((b, s, D_MODEL), BF16),
        scratch_shapes=[
            pltpu.VMEM((2 * t, 1), F32),
            pltpu.VMEM((2 * t, 1), F32),
            pltpu.VMEM((2 * t, HEAD_WIDTH), F32),
        ],
        compiler_params=_params(("parallel", "parallel", "parallel")),
        name="diff_attention",
    )(lamv, subln, bias, q, k, v)


def _s5_discretise(lam_re, lam_im, log_step, b_re, b_im, c_re, c_im):
    step = jnp.exp(log_step.astype(F32))[:, None]
    lr, li = lam_re.astype(F32), lam_im.astype(F32)
    mag = jnp.exp(lr * step)
    ab_re, ab_im = mag * jnp.cos(li * step), mag * jnp.sin(li * step)
    den = lr * lr + li * li
    nr, ni = ab_re - 1.0, ab_im
    f_re = (nr * lr + ni * li) / den
    f_im = (ni * lr - nr * li) / den
    br, bi = b_re.astype(F32), b_im.astype(F32)
    bb_re = f_re[..., None] * br - f_im[..., None] * bi
    bb_im = f_re[..., None] * bi + f_im[..., None] * br
    gs = MXU_DIM // S5_GROUP
    eye = jnp.eye(gs, dtype=F32)

    def in_blocks(bb):
        blk = jnp.einsum('kgph,gj->kghjp', bb.reshape(S5_SLABS, gs, S5_STATE, S5_GROUP), eye)
        return blk.reshape(S5_SLABS, MXU_DIM, S5_SLAB_STATE)

    def out_blocks(c):
        blk = jnp.einsum('kghp,gj->kgpjh', c.reshape(S5_SLABS, gs, S5_GROUP, S5_STATE), eye)
        return blk.reshape(S5_SLABS, S5_SLAB_STATE, MXU_DIM)

    b_blk = jnp.concatenate([in_blocks(bb_re), in_blocks(bb_im)], axis=2).astype(BF16)
    c_blk = jnp.concatenate([out_blocks(c_re.astype(F32)), -out_blocks(c_im.astype(F32))],
                            axis=1).astype(BF16)
    a = jnp.stack([ab_re.reshape(S5_SLABS, S5_SLAB_STATE), ab_im.reshape(S5_SLABS, S5_SLAB_STATE)],
                  axis=1)
    return a, b_blk, c_blk


def _s5_kernel(u_ref, a_ref, b_ref, c_ref, d_ref, y_ref, x_scr, st_scr, *, batch):
    w = S5_SLAB_STATE

    @pl.when(pl.program_id(0) == 0)
    def _():
        st_scr[...] = jnp.zeros(st_scr.shape, F32)

    u = u_ref[...]
    ub = u.astype(BF16)
    for kt in range(S5_SLABS):
        x_scr[:, kt * 2 * w:(kt + 1) * 2 * w] = jnp.dot(
            ub[:, kt * MXU_DIM:(kt + 1) * MXU_DIM], b_ref[kt], preferred_element_type=F32)

    for kt in range(S5_SLABS):
        re = slice(kt * 2 * w, kt * 2 * w + w)
        im = slice(kt * 2 * w + w, (kt + 1) * 2 * w)
        ar = jnp.broadcast_to(a_ref[kt, 0:1, :], (batch, w))
        ai = jnp.broadcast_to(a_ref[kt, 1:2, :], (batch, w))

        def step(t, carry, re=re, im=im, ar=ar, ai=ai):
            xr, xi = carry
            rows = pl.ds(pl.multiple_of(t * batch, batch), batch)
            nxr = ar * xr - ai * xi + x_scr[rows, re]
            nxi = ar * xi + ai * xr + x_scr[rows, im]
            x_scr[rows, re] = nxr
            x_scr[rows, im] = nxi
            return nxr, nxi

        xr, xi = lax.fori_loop(0, S5_STEPS, step, (st_scr[:, re], st_scr[:, im]), unroll=4)
        st_scr[:, re] = xr
        st_scr[:, im] = xi

    for kt in range(S5_SLABS):
        cols = slice(kt * MXU_DIM, (kt + 1) * MXU_DIM)
        y = jnp.dot(x_scr[:, kt * 2 * w:(kt + 1) * 2 * w].astype(BF16), c_ref[kt],
                    preferred_element_type=F32)
        y = y + d_ref[:, cols] * u[:, cols]
        y_ref[:, cols] = jax.nn.gelu(y).astype(BF16)


def _s5(u_tm, a, b_blk, c_blk, d, batch, seq):
    rows = S5_STEPS * batch
    w = S5_SLAB_STATE
    return pl.pallas_call(
        functools.partial(_s5_kernel, batch=batch),
        grid=(seq // S5_STEPS,),
        in_specs=[
            pl.BlockSpec((rows, D_MODEL), lambda c: (c, 0)),
            pl.BlockSpec((S5_SLABS, 2, w), lambda c: (0, 0, 0)),
            pl.BlockSpec((S5_SLABS, MXU_DIM, 2 * w), lambda c: (0, 0, 0)),
            pl.BlockSpec((S5_SLABS, 2 * w, MXU_DIM), lambda c: (0, 0, 0)),
            pl.BlockSpec((1, D_MODEL), lambda c: (0, 0)),
        ],
        out_specs=pl.BlockSpec((rows, D_MODEL), lambda c: (c, 0)),
        out_shape=jax.ShapeDtypeStruct((seq * batch, D_MODEL), BF16),
        scratch_shapes=[
            pltpu.VMEM((rows, S5_SLABS * 2 * w), F32),
            pltpu.VMEM((batch, S5_SLABS * 2 * w), F32),
        ],
        compiler_params=_params(("arbitrary",)),
        name="s5_scan",
    )(u_tm, a, b_blk, c_blk, d)


def _conv_kernel(c_ref, halo_ref, w_ref, b_ref, lg_ref, lb_ref, o_ref, xe_scr, y_scr, *, nsb):
    tm = c_ref.shape[0]
    first = (pl.program_id(0) % nsb) == 0
    xe_scr[0:CONV_HALO, :] = jnp.where(first, 0.0, halo_ref[...].astype(F32))
    xe_scr[CONV_HALO:, :] = c_ref[...].astype(F32)
    lead = CONV_HALO - (CONV_KERNEL - 1)
    for cb in range(D_MODEL // LANES):
        cols = slice(cb * LANES, (cb + 1) * LANES)
        wcol = w_ref[:, cols]
        bcol = b_ref[:, cols]
        for r0 in range(0, tm, CONV_ROWS):
            acc = jnp.broadcast_to(bcol, (CONV_ROWS, LANES))
            for k in range(CONV_KERNEL):
                acc = acc + wcol[k:k + 1, :] * xe_scr[r0 + lead + k:r0 + lead + k + CONV_ROWS, cols]
            y_scr[r0:r0 + CONV_ROWS, cols] = acc
    y = y_scr[...]
    yc = y - jnp.mean(y, axis=-1, keepdims=True)
    yn = yc * lax.rsqrt(jnp.mean(yc * yc, axis=-1, keepdims=True) + EPS) * lg_ref[...] + lb_ref[...]
    o_ref[...] = (yn * jax.nn.sigmoid(yn)).astype(BF16)


def _conv_module(c, w, b, lg, lb, seq):
    n = c.shape[0]
    tm = CONV_TILE
    nsb = seq // tm
    hb = tm // CONV_HALO
    vec = pl.BlockSpec((1, D_MODEL), lambda i: (0, 0))
    return pl.pallas_call(
        functools.partial(_conv_kernel, nsb=nsb),
        grid=(n // tm,),
        in_specs=[
            pl.BlockSpec((tm, D_MODEL), lambda i: (i, 0)),
            pl.BlockSpec((CONV_HALO, D_MODEL), lambda i: (jnp.maximum(i * hb - 1, 0), 0)),
            pl.BlockSpec((CONV_KERNEL, D_MODEL), lambda i: (0, 0)),
            vec, vec, vec,
        ],
        out_specs=pl.BlockSpec((tm, D_MODEL), lambda i: (i, 0)),
        out_shape=jax.ShapeDtypeStruct((n, D_MODEL), BF16),
        scratch_shapes=[pltpu.VMEM((tm + CONV_HALO, D_MODEL), F32), pltpu.VMEM((tm, D_MODEL), F32)],
        compiler_params=_params(("parallel",)),
        name="conv_module",
    )(c, c, w, b, lg, lb)


def _merge_kernel(o_ref, y_ref, c_ref, g_ref, x_ref, wa_ref, w1_ref, w2_ref, wc_ref, wo_ref, out_ref):
    dot = functools.partial(jnp.dot, preferred_element_type=F32)
    y = y_ref[...]
    mix = g_ref[:, 0:D_MODEL].astype(F32) * dot(o_ref[...], wa_ref[...])
    mix = mix + g_ref[:, D_MODEL:2 * D_MODEL].astype(F32) * (
        dot(y, w1_ref[...]) * jax.nn.sigmoid(dot(y, w2_ref[...])))
    mix = mix + g_ref[:, 2 * D_MODEL:].astype(F32) * dot(c_ref[...], wc_ref[...])
    out_ref[...] = x_ref[...] + dot(mix.astype(BF16), wo_ref[...])


def _merge(o, y_tm, c, gates, x2d, wa, w1, w2, wc, wo, seq):
    n = x2d.shape[0]
    tm = ROW_TILE
    nsb = seq // tm
    row = pl.BlockSpec((tm, D_MODEL), lambda i: (i, 0))
    wgt = pl.BlockSpec((D_MODEL, D_MODEL), lambda i: (0, 0))
    return pl.pallas_call(
        _merge_kernel,
        grid=(n // tm,),
        in_specs=[
            row,
            pl.BlockSpec((tm, D_MODEL), lambda i: (i % nsb, i // nsb)),
            row,
            pl.BlockSpec((tm, 3 * D_MODEL), lambda i: (i, 0)),
            row,
            wgt, wgt, wgt, wgt, wgt,
        ],
        out_specs=row,
        out_shape=jax.ShapeDtypeStruct((n, D_MODEL), F32),
        compiler_params=_params(("parallel",)),
        name="merge",
    )(o, y_tm, c, gates, x2d, wa, w1, w2, wc, wo)


def _ffn_kernel(x_ref, xh_ref, g_ref, wa_ref, wb_ref, dwa_ref, dwb_ref, wd_ref, out_ref,
                h_scr, ua_scr, ub_scr, acc_scr, *, nsb):
    tm = x_ref.shape[0]
    first = (pl.program_id(0) % nsb) == 0
    g = g_ref[...]
    h_scr[0:FFN_HALO, :] = jnp.where(first, 0.0, _rms_rows(xh_ref[...], g)).astype(BF16)
    h_scr[FFN_HALO:, :] = _rms_rows(x_ref[...], g).astype(BF16)
    lead = FFN_HALO - (FFN_KERNEL - 1)

    def taps(scr, dw):
        out = dw[0:1, :] * scr[lead:lead + tm, :]
        for k in range(1, FFN_KERNEL):
            out = out + dw[k:k + 1, :] * scr[lead + k:lead + k + tm, :]
        return out

    for j in range(FFN_TILES):
        h = h_scr[...]
        ua_scr[...] = jnp.dot(h, wa_ref[j], preferred_element_type=F32)
        ub_scr[...] = jnp.dot(h, wb_ref[j], preferred_element_type=F32)
        act = jax.nn.gelu(taps(ub_scr, dwb_ref[j])) * taps(ua_scr, dwa_ref[j])
        part = jnp.dot(act.astype(BF16), wd_ref[j], preferred_element_type=F32)
        if j == 0:
            acc_scr[...] = part
        else:
            acc_scr[...] += part
    out_ref[...] = x_ref[...] + acc_scr[...]


def _ffn(x2d, g, wa, wb, dwa, dwb, wd, seq):
    n = x2d.shape[0]
    tm = ROW_TILE
    nsb = seq // tm
    hb = tm // FFN_HALO
    const3 = lambda i: (0, 0, 0)
    return pl.pallas_call(
        functools.partial(_ffn_kernel, nsb=nsb),
        grid=(n // tm,),
        in_specs=[
            pl.BlockSpec((tm, D_MODEL), lambda i: (i, 0)),
            pl.BlockSpec((FFN_HALO, D_MODEL), lambda i: (jnp.maximum(i * hb - 1, 0), 0)),
            pl.BlockSpec((1, D_MODEL), lambda i: (0, 0)),
            pl.BlockSpec((FFN_TILES, D_MODEL, FFN_TILE), const3),
            pl.BlockSpec((FFN_TILES, D_MODEL, FFN_TILE), const3),
            pl.BlockSpec((FFN_TILES, FFN_KERNEL, FFN_TILE), const3),
            pl.BlockSpec((FFN_TILES, FFN_KERNEL, FFN_TILE), const3),
            pl.BlockSpec((FFN_TILES, FFN_TILE, D_MODEL), const3),
        ],
        out_specs=pl.BlockSpec((tm, D_MODEL), lambda i: (i, 0)),
        out_shape=jax.ShapeDtypeStruct((n, D_MODEL), F32),
        scratch_shapes=[
            pltpu.VMEM((tm + FFN_HALO, D_MODEL), BF16),
            pltpu.VMEM((tm + FFN_HALO, FFN_TILE), F32),
            pltpu.VMEM((tm + FFN_HALO, FFN_TILE), F32),
            pltpu.VMEM((tm, D_MODEL), F32),
        ],
        compiler_params=_params(("parallel",)),
        name="ffn",
    )(x2d, x2d, g, wa, wb, dwa, dwb, wd)


def _ffn_tiles(w_up, dw, w_down):
    def cols(m, half):
        part = m[:, half * FFN_HIDDEN:(half + 1) * FFN_HIDDEN]
        return part.reshape(m.shape[0], FFN_TILES, FFN_TILE).transpose(1, 0, 2)

    return (cols(w_up, 0).astype(BF16), cols(w_up, 1).astype(BF16), cols(dw, 0), cols(dw, 1),
            w_down.reshape(FFN_TILES, FFN_TILE, D_MODEL).astype(BF16))


def kernel(x, norm_mix, w_in, qk_gain_q, qk_gain_k, lambda_q1, lambda_k1, lambda_q2, lambda_k2, diff_subln, rel_bias, w_attn_out, s5_lambda_re, s5_lambda_im, s5_log_step, s5_b_re, s5_b_im, s5_c_re, s5_c_im, s5_d, s5_glu_w1, s5_glu_w2, conv_dw_w, conv_dw_b, conv_ln_g, conv_ln_b, conv_w_out, w_out, norm_ffn, ffn_w_up, ffn_dw_w, ffn_w_down):
    batch, seq, _ = x.shape
    assert batch == SUBLANES, "the S5 scan keeps one batch entry per sublane"
    n = batch * seq
    row = lambda p: p.reshape(1, -1).astype(F32)
    bias = _bias_tiles(rel_bias.astype(F32))
    x2d = x.reshape(n, D_MODEL)
    for l in range(DEPTH):
        lam_init = 0.8 - 0.6 * math.exp(-0.3 * l)
        q, k, v, u_tm, c_glu, gates = _in_proj(
            x2d, row(norm_mix[l]), w_in[l].astype(BF16),
            row(jnp.tile(qk_gain_q[l], 2)), row(jnp.tile(qk_gain_k[l], 2)), batch, seq)

        lamv = jnp.stack([lambda_q1[l], lambda_k1[l], lambda_q2[l], lambda_k2[l]]).astype(F32)
        shape3 = (batch, seq, D_MODEL)
        o = _attention(q.reshape(shape3), k.reshape(shape3), v.reshape(shape3), bias, lamv,
                       row(diff_subln[l]), lam_init)

        a, b_blk, c_blk = _s5_discretise(s5_lambda_re[l], s5_lambda_im[l], s5_log_step[l],
                                         s5_b_re[l], s5_b_im[l], s5_c_re[l], s5_c_im[l])
        y_tm = _s5(u_tm.reshape(seq * batch, D_MODEL), a, b_blk, c_blk, row(s5_d[l]), batch, seq)

        c = _conv_module(c_glu, conv_dw_w[l].astype(F32), row(conv_dw_b[l]), row(conv_ln_g[l]),
                         row(conv_ln_b[l]), seq)

        x2d = _merge(o.reshape(n, D_MODEL), y_tm.reshape(seq, batch * D_MODEL), c, gates, x2d,
                     w_attn_out[l].astype(BF16), s5_glu_w1[l].astype(BF16),
                     s5_glu_w2[l].astype(BF16), conv_w_out[l].astype(BF16), w_out[l].astype(BF16),
                     seq)

        x2d = _ffn(x2d, row(norm_ffn[l]), *_ffn_tiles(ffn_w_up[l], ffn_dw_w[l], ffn_w_down[l]), seq)
    return x2d.reshape(batch, seq, D_MODEL)
```

```python
import functools
import math

import jax
import jax.numpy as jnp
from jax import lax
from jax.experimental import pallas as pl
from jax.experimental.pallas import tpu as pltpu

F32 = jnp.float32
BF16 = jnp.bfloat16

D_MODEL = 1024
DEPTH = 2
CHUNK = 64
HEADS = 8
HEAD_DIM = 64
HEAD_WIDTH = 2 * HEAD_DIM
S5_GROUP = 16
S5_GROUPS = 64
S5_STATE = 64
CONV_KERNEL = 31
FFN_HIDDEN = 2816
FFN_KERNEL = 3
REL_BUCKETS = 32
REL_MAX_DIST = 128
EPS = 1e-6
IN_TILES = 9
NEG_INF = -1e30

LANES = 128
SUBLANES = 8
MXU_DIM = 256
VMEM_LIMIT = 56 * 1024 * 1024

ROW_TILE = 512
ATTN_TILE = 256
ATTN_UNROLL = 4
S5_STEPS = 32
S5_SLABS = D_MODEL // MXU_DIM
S5_SLAB_STATE = (MXU_DIM // S5_GROUP) * S5_STATE
CONV_HALO = 32
CONV_TILE = 256
FFN_HALO = 16
FFN_TILE = 256
FFN_TILES = FFN_HIDDEN // FFN_TILE


def _params(semantics):
    return pltpu.CompilerParams(dimension_semantics=semantics, vmem_limit_bytes=VMEM_LIMIT)


def _rms_rows(x, g):
    return x * lax.rsqrt(jnp.mean(x * x, axis=-1, keepdims=True) + EPS) * g


def _head_norm(z, gain, scale):
    lane = lax.broadcasted_iota(jnp.int32, (1, HEAD_WIDTH), 1)
    first = lane < HEAD_DIM
    outs = []
    for h in range(z.shape[1] // HEAD_WIDTH):
        slab = z[:, h * HEAD_WIDTH:(h + 1) * HEAD_WIDTH]
        sq = slab * slab
        s0 = jnp.sum(jnp.where(first, sq, 0.0), axis=-1, keepdims=True)
        s1 = jnp.sum(jnp.where(first, 0.0, sq), axis=-1, keepdims=True)
        r = lax.rsqrt(jnp.where(first, s0, s1) * (1.0 / HEAD_DIM) + EPS)
        outs.append((slab * r * gain * scale).astype(BF16))
    return jnp.concatenate(outs, axis=-1)


def _in_proj_kernel(x_ref, g_ref, w_ref, gq_ref, gk_ref,
                    q_ref, k_ref, v_ref, u_ref, c_ref, gate_ref, h_scr, a_scr):
    j = pl.program_id(1)

    @pl.when(j == 0)
    def _():
        h_scr[...] = _rms_rows(x_ref[...], g_ref[...]).astype(BF16)

    def project(epilogue):
        for c in range(D_MODEL // MXU_DIM):
            cols = slice(c * MXU_DIM, (c + 1) * MXU_DIM)
            epilogue(cols, jnp.dot(h_scr[...], w_ref[:, cols], preferred_element_type=F32))

    def store(ref, fn):
        def epilogue(cols, z):
            ref[:, cols] = fn(cols, z)
        return epilogue

    @pl.when(j == 0)
    def _():
        project(store(q_ref, lambda cols, z: _head_norm(z, gq_ref[...], HEAD_DIM ** -0.5)))

    @pl.when(j == 1)
    def _():
        project(store(k_ref, lambda cols, z: _head_norm(z, gk_ref[...], 1.0)))

    @pl.when(j == 2)
    def _():
        project(store(v_ref, lambda cols, z: z.astype(BF16)))

    @pl.when(j == 3)
    def _():
        project(store(u_ref, lambda cols, z: z))

    @pl.when(j == 4)
    def _():
        project(store(a_scr, lambda cols, z: z))

    @pl.when(j == 5)
    def _():
        project(store(c_ref, lambda cols, z: (a_scr[:, cols] * jax.nn.sigmoid(z)).astype(BF16)))

    @pl.when(j >= 6)
    def _():
        project(store(gate_ref, lambda cols, z: jax.nn.sigmoid(z).astype(BF16)))


def _in_proj(x2d, g, w, gq, gk, batch, seq):
    n = x2d.shape[0]
    tm = ROW_TILE
    nsb = seq // tm
    row = lambda i, j: (i, 0)
    return pl.pallas_call(
        _in_proj_kernel,
        grid=(n // tm, IN_TILES),
        in_specs=[
            pl.BlockSpec((tm, D_MODEL), row),
            pl.BlockSpec((1, D_MODEL), lambda i, j: (0, 0)),
            pl.BlockSpec((D_MODEL, D_MODEL), lambda i, j: (0, j)),
            pl.BlockSpec((1, HEAD_WIDTH), lambda i, j: (0, 0)),
            pl.BlockSpec((1, HEAD_WIDTH), lambda i, j: (0, 0)),
        ],
        out_specs=[
            pl.BlockSpec((tm, D_MODEL), row),
            pl.BlockSpec((tm, D_MODEL), row),
            pl.BlockSpec((tm, D_MODEL), row),
            pl.BlockSpec((tm, D_MODEL), lambda i, j: (i % nsb, i // nsb)),
            pl.BlockSpec((tm, D_MODEL), row),
            pl.BlockSpec((tm, D_MODEL), lambda i, j: (i, jnp.maximum(j - 6, 0))),
        ],
        out_shape=[
            jax.ShapeDtypeStruct((n, D_MODEL), BF16),
            jax.ShapeDtypeStruct((n, D_MODEL), BF16),
            jax.ShapeDtypeStruct((n, D_MODEL), BF16),
            jax.ShapeDtypeStruct((seq, batch * D_MODEL), F32),
            jax.ShapeDtypeStruct((n, D_MODEL), BF16),
            jax.ShapeDtypeStruct((n, 3 * D_MODEL), BF16),
        ],
        scratch_shapes=[pltpu.VMEM((tm, D_MODEL), BF16), pltpu.VMEM((tm, D_MODEL), F32)],
        compiler_params=_params(("parallel", "arbitrary")),
        name="in_proj",
    )(x2d, g, w, gq, gk)


def _t5_bucket_ids(rel):
    nb = REL_BUCKETS // 2
    n = -rel
    ret = jnp.where(n < 0, nb, 0)
    n = jnp.abs(n)
    max_exact = nb // 2
    nf = jnp.maximum(n, 1).astype(F32)
    large = max_exact + (jnp.log(nf / max_exact) / math.log(REL_MAX_DIST / max_exact)
                         * (nb - max_exact)).astype(jnp.int32)
    large = jnp.minimum(large, nb - 1)
    return ret + jnp.where(n < max_exact, n, large)


def _bias_kernel(far_ref, tab_ref, ids_ref, out_ref):
    h = pl.program_id(0)
    t = ATTN_TILE
    ids = ids_ref[...]
    acc = jnp.zeros(ids.shape, F32)
    for b in range(REL_BUCKETS):
        acc = jnp.where(ids == b, tab_ref[b, h], acc)
    acc = acc - tab_ref[far_ref[0], h]
    qi = lax.broadcasted_iota(jnp.int32, (t, t), 0)
    kj = lax.broadcasted_iota(jnp.int32, (t, t), 1)
    visible = (kj // CHUNK) <= (qi // CHUNK)
    diag = jnp.where(visible, acc[0], NEG_INF)
    out_ref[0] = jnp.zeros((2 * t, t), F32)
    out_ref[1, 0:t] = acc[1]
    out_ref[1, t:2 * t] = acc[1]
    out_ref[2, 0:t] = diag
    out_ref[2, t:2 * t] = diag


def _bias_tiles(rel_bias):
    t = ATTN_TILE
    qi = jnp.arange(t, dtype=jnp.int32)[:, None]
    kj = jnp.arange(t, dtype=jnp.int32)[None, :]
    ids = jnp.stack([_t5_bucket_ids(kj - qi), _t5_bucket_ids(kj - qi - t)])
    far = _t5_bucket_ids(jnp.full((1,), -(t + 1), jnp.int32))
    return pl.pallas_call(
        _bias_kernel,
        grid=(HEADS,),
        in_specs=[
            pl.BlockSpec(memory_space=pltpu.SMEM),
            pl.BlockSpec(memory_space=pltpu.SMEM),
            pl.BlockSpec((2, t, t), lambda h: (0, 0, 0)),
        ],
        out_specs=pl.BlockSpec((None, 3, 2 * t, t), lambda h: (h, 0, 0, 0)),
        out_shape=jax.ShapeDtypeStruct((HEADS, 3, 2 * t, t), F32),
        compiler_params=_params(("parallel",)),
        name="bias_tiles",
    )(far, rel_bias, ids)


def _attn_kernel(lamv_ref, subln_ref, bias_ref, q_ref, k_ref, v_ref, o_ref,
                 s_scr, m_scr, l_scr, acc_scr, *, lam_init):
    t = ATTN_TILE
    i = pl.program_id(2)
    n_tiles = i + 1
    n_groups = n_tiles // ATTN_UNROLL
    q = q_ref[...]
    lane = lax.broadcasted_iota(jnp.int32, (1, HEAD_WIDTH), 1)
    first = lane < HEAD_DIM
    zero = jnp.zeros_like(q)
    qs = jnp.concatenate([jnp.where(first, q, zero), jnp.where(first, zero, q)], axis=0)

    def rows(j):
        return pl.ds(pl.multiple_of(j * t, t), t)

    def score_tile(j):
        s = lax.dot_general(qs, k_ref[rows(j), :], (((1,), (1,)), ((), ())),
                            preferred_element_type=F32)
        s = s + bias_ref[jnp.clip(j - (i - 2), 0, 2)]
        s_scr[j] = s
        return jnp.maximum(s[:, 0:LANES], s[:, LANES:2 * LANES])

    def max_group(g, m):
        for u in range(ATTN_UNROLL):
            m = jnp.maximum(m, score_tile(g * ATTN_UNROLL + u))
        return m

    m = jnp.full((2 * t, LANES), NEG_INF, F32)
    m = lax.fori_loop(0, n_groups, max_group, m)
    m = lax.fori_loop(n_groups * ATTN_UNROLL, n_tiles, lambda j, m: jnp.maximum(m, score_tile(j)), m)
    m_scr[...] = jnp.broadcast_to(jnp.max(m, axis=-1, keepdims=True), (2 * t, LANES))
    l_scr[...] = jnp.zeros(l_scr.shape, F32)
    acc_scr[...] = jnp.zeros(acc_scr.shape, F32)

    def prob_tile(j):
        m_rep = m_scr[...]
        p = jnp.exp(s_scr[j] - jnp.concatenate([m_rep, m_rep], axis=1))
        pv = jnp.dot(p.astype(BF16), v_ref[rows(j), :], preferred_element_type=F32)
        return p[:, 0:LANES] + p[:, LANES:2 * LANES], pv

    def sum_group(g, carry):
        parts = [prob_tile(g * ATTN_UNROLL + u) for u in range(ATTN_UNROLL)]
        l_scr[...] += functools.reduce(lambda a, b: a + b, [dl for dl, _ in parts])
        acc_scr[...] += functools.reduce(lambda a, b: a + b, [pv for _, pv in parts])
        return carry

    def sum_tile(j, carry):
        dl, pv = prob_tile(j)
        l_scr[...] += dl
        acc_scr[...] += pv
        return carry

    lax.fori_loop(0, n_groups, sum_group, 0)
    lax.fori_loop(n_groups * ATTN_UNROLL, n_tiles, sum_tile, 0)

    lv = lamv_ref[...]
    lam = (jnp.exp(jnp.sum(lv[0:1] * lv[1:2], axis=-1, keepdims=True))
           - jnp.exp(jnp.sum(lv[2:3] * lv[3:4], axis=-1, keepdims=True)) + lam_init)
    l = jnp.sum(l_scr[...], axis=-1, keepdims=True)
    o = acc_scr[0:t] / l[0:t] - lam * (acc_scr[t:2 * t] / l[t:2 * t])
    o_ref[...] = (_rms_rows(o, subln_ref[...]) * (1.0 - lam_init)).astype(BF16)


def _attention(q, k, v, bias, lamv, subln, lam_init):
    b, s, _ = q.shape
    t = ATTN_TILE
    return pl.pallas_call(
        functools.partial(_attn_kernel, lam_init=lam_init),
        grid=(b, HEADS, s // t),
        in_specs=[
            pl.BlockSpec((4, HEAD_DIM), lambda bi, h, i: (0, 0)),
            pl.BlockSpec((1, HEAD_WIDTH), lambda bi, h, i: (0, 0)),
            pl.BlockSpec((None, 3, 2 * t, t), lambda bi, h, i: (h, 0, 0, 0)),
            pl.BlockSpec((None, t, HEAD_WIDTH), lambda bi, h, i: (bi, i, h)),
            pl.BlockSpec((None, s, HEAD_WIDTH), lambda bi, h, i: (bi, 0, h)),
            pl.BlockSpec((None, s, HEAD_WIDTH), lambda bi, h, i: (bi, 0, h)),
        ],
        out_specs=pl.BlockSpec((None, t, HEAD_WIDTH), lambda bi, h, i: (bi, i, h)),
        out_shape=jax.ShapeDtypeStruct((b, s, D_MODEL), BF16),
        scratch_shapes=[
            pltpu.VMEM((s // t, 2 * t, t), F32),
            pltpu.VMEM((2 * t, LANES), F32),
            pltpu.VMEM((2 * t, LANES), F32),
            pltpu.VMEM((2 * t, HEAD_WIDTH), F32),
        ],
        compiler_params=_params(("parallel", "parallel", "parallel")),
        name="diff_attention",
    )(lamv, subln, bias, q, k, v)


def _s5_discretise(lam_re, lam_im, log_step, b_re, b_im, c_re, c_im):
    step = jnp.exp(log_step.astype(F32))[:, None]
    lr, li = lam_re.astype(F32), lam_im.astype(F32)
    mag = jnp.exp(lr * step)
    ab_re, ab_im = mag * jnp.cos(li * step), mag * jnp.sin(li * step)
    den = lr * lr + li * li
    nr, ni = ab_re - 1.0, ab_im
    f_re = (nr * lr + ni * li) / den
    f_im = (ni * lr - nr * li) / den
    br, bi = b_re.astype(F32), b_im.astype(F32)
    bb_re = f_re[..., None] * br - f_im[..., None] * bi
    bb_im = f_re[..., None] * bi + f_im[..., None] * br
    gs = MXU_DIM // S5_GROUP
    eye = jnp.eye(gs, dtype=F32)

    def in_blocks(bb):
        blk = jnp.einsum('kgph,gj->kghjp', bb.reshape(S5_SLABS, gs, S5_STATE, S5_GROUP), eye)
        return blk.reshape(S5_SLABS, MXU_DIM, S5_SLAB_STATE)

    def out_blocks(c):
        blk = jnp.einsum('kghp,gj->kgpjh', c.reshape(S5_SLABS, gs, S5_GROUP, S5_STATE), eye)
        return blk.reshape(S5_SLABS, S5_SLAB_STATE, MXU_DIM)

    b_blk = jnp.concatenate([in_blocks(bb_re), in_blocks(bb_im)], axis=2).astype(BF16)
    c_blk = jnp.concatenate([out_blocks(c_re.astype(F32)), -out_blocks(c_im.astype(F32))],
                            axis=1).astype(BF16)
    a = jnp.stack([ab_re.reshape(S5_SLABS, S5_SLAB_STATE), ab_im.reshape(S5_SLABS, S5_SLAB_STATE)],
                  axis=1)
    return a, b_blk, c_blk


def _s5_kernel(u_ref, a_ref, b_ref, c_ref, d_ref, y_ref, x_scr, st_scr, *, batch):
    w = S5_SLAB_STATE

    @pl.when(pl.program_id(0) == 0)
    def _():
        st_scr[...] = jnp.zeros(st_scr.shape, F32)

    u = u_ref[...]
    ub = u.astype(BF16)
    for kt in range(S5_SLABS):
        x_scr[:, kt * 2 * w:(kt + 1) * 2 * w] = jnp.dot(
            ub[:, kt * MXU_DIM:(kt + 1) * MXU_DIM], b_ref[kt], preferred_element_type=F32)

    for kt in range(S5_SLABS):
        re = slice(kt * 2 * w, kt * 2 * w + w)
        im = slice(kt * 2 * w + w, (kt + 1) * 2 * w)
        ar = jnp.broadcast_to(a_ref[kt, 0:1, :], (batch, w))
        ai = jnp.broadcast_to(a_ref[kt, 1:2, :], (batch, w))

        def step(t, carry, re=re, im=im, ar=ar, ai=ai):
            xr, xi = carry
            rows = pl.ds(pl.multiple_of(t * batch, batch), batch)
            nxr = ar * xr - ai * xi + x_scr[rows, re]
            nxi = ar * xi + ai * xr + x_scr[rows, im]
            x_scr[rows, re] = nxr
            x_scr[rows, im] = nxi
            return nxr, nxi

        xr, xi = lax.fori_loop(0, S5_STEPS, step, (st_scr[:, re], st_scr[:, im]), unroll=4)
        st_scr[:, re] = xr
        st_scr[:, im] = xi

    for kt in range(S5_SLABS):
        cols = slice(kt * MXU_DIM, (kt + 1) * MXU_DIM)
        y = jnp.dot(x_scr[:, kt * 2 * w:(kt + 1) * 2 * w].astype(BF16), c_ref[kt],
                    preferred_element_type=F32)
        y = y + d_ref[:, cols] * u[:, cols]
        y_ref[:, cols] = jax.nn.gelu(y).astype(BF16)


def _s5(u_tm, a, b_blk, c_blk, d, batch, seq):
    rows = S5_STEPS * batch
    w = S5_SLAB_STATE
    return pl.pallas_call(
        functools.partial(_s5_kernel, batch=batch),
        grid=(seq // S5_STEPS,),
        in_specs=[
            pl.BlockSpec((rows, D_MODEL), lambda c: (c, 0)),
            pl.BlockSpec((S5_SLABS, 2, w), lambda c: (0, 0, 0)),
            pl.BlockSpec((S5_SLABS, MXU_DIM, 2 * w), lambda c: (0, 0, 0)),
            pl.BlockSpec((S5_SLABS, 2 * w, MXU_DIM), lambda c: (0, 0, 0)),
            pl.BlockSpec((1, D_MODEL), lambda c: (0, 0)),
        ],
        out_specs=pl.BlockSpec((rows, D_MODEL), lambda c: (c, 0)),
        out_shape=jax.ShapeDtypeStruct((seq * batch, D_MODEL), BF16),
        scratch_shapes=[
            pltpu.VMEM((rows, S5_SLABS * 2 * w), F32),
            pltpu.VMEM((batch, S5_SLABS * 2 * w), F32),
        ],
        compiler_params=_params(("arbitrary",)),
        name="s5_scan",
    )(u_tm, a, b_blk, c_blk, d)


def _conv_kernel(c_ref, halo_ref, w_ref, b_ref, lg_ref, lb_ref, o_ref,
                 xe_scr, xs_scr, wb_scr, y_scr, *, nsb):
    tm = c_ref.shape[0]
    strips = D_MODEL // LANES
    first = (pl.program_id(0) % nsb) == 0
    xe_scr[0:CONV_HALO, :] = jnp.where(first, 0.0, halo_ref[...].astype(F32))
    xe_scr[CONV_HALO:, :] = c_ref[...].astype(F32)
    for b in range(SUBLANES):
        span = tm + CONV_HALO - (SUBLANES if b else 0)
        for cb in range(strips):
            xs_scr[b, cb, 0:span, :] = xe_scr[b:b + span, cb * LANES:(cb + 1) * LANES]
    for cb in range(strips):
        cols = slice(cb * LANES, (cb + 1) * LANES)
        for k in range(CONV_KERNEL):
            wb_scr[k, cb] = jnp.broadcast_to(w_ref[k:k + 1, cols], (SUBLANES, LANES))
        wb_scr[CONV_KERNEL, cb] = jnp.broadcast_to(b_ref[:, cols], (SUBLANES, LANES))
    lead = CONV_HALO - (CONV_KERNEL - 1)
    reps = tm // SUBLANES

    def strip(cb, carry):
        acc = jnp.tile(wb_scr[CONV_KERNEL, cb], (reps, 1))
        for k in range(CONV_KERNEL):
            shift = (lead + k) % SUBLANES
            base = lead + k - shift
            acc = acc + jnp.tile(wb_scr[k, cb], (reps, 1)) * xs_scr[shift, cb, base:base + tm, :]
        y_scr[cb] = acc
        return carry

    lax.fori_loop(0, strips, strip, 0)
    y = jnp.concatenate([y_scr[cb] for cb in range(strips)], axis=1)
    yc = y - jnp.mean(y, axis=-1, keepdims=True)
    yn = yc * lax.rsqrt(jnp.mean(yc * yc, axis=-1, keepdims=True) + EPS) * lg_ref[...] + lb_ref[...]
    o_ref[...] = (yn * jax.nn.sigmoid(yn)).astype(BF16)


def _conv_module(c, w, b, lg, lb, seq):
    n = c.shape[0]
    tm = CONV_TILE
    nsb = seq // tm
    hb = tm // CONV_HALO
    vec = pl.BlockSpec((1, D_MODEL), lambda i: (0, 0))
    return pl.pallas_call(
        functools.partial(_conv_kernel, nsb=nsb),
        grid=(n // tm,),
        in_specs=[
            pl.BlockSpec((tm, D_MODEL), lambda i: (i, 0)),
            pl.BlockSpec((CONV_HALO, D_MODEL), lambda i: (jnp.maximum(i * hb - 1, 0), 0)),
            pl.BlockSpec((CONV_KERNEL, D_MODEL), lambda i: (0, 0)),
            vec, vec, vec,
        ],
        out_specs=pl.BlockSpec((tm, D_MODEL), lambda i: (i, 0)),
        out_shape=jax.ShapeDtypeStruct((n, D_MODEL), BF16),
        scratch_shapes=[pltpu.VMEM((tm + CONV_HALO, D_MODEL), F32),
                        pltpu.VMEM((SUBLANES, D_MODEL // LANES, tm + CONV_HALO, LANES), F32),
                        pltpu.VMEM((CONV_KERNEL + 1, D_MODEL // LANES, SUBLANES, LANES), F32),
                        pltpu.VMEM((D_MODEL // LANES, tm, LANES), F32)],
        compiler_params=_params(("parallel",)),
        name="conv_module",
    )(c, c, w, b, lg, lb)


def _merge_kernel(o_ref, y_ref, c_ref, g_ref, x_ref, wa_ref, w1_ref, w2_ref, wc_ref, wo_ref, out_ref):
    dot = functools.partial(jnp.dot, preferred_element_type=F32)
    y = y_ref[...]
    mix = g_ref[:, 0:D_MODEL].astype(F32) * dot(o_ref[...], wa_ref[...])
    mix = mix + g_ref[:, D_MODEL:2 * D_MODEL].astype(F32) * (
        dot(y, w1_ref[...]) * jax.nn.sigmoid(dot(y, w2_ref[...])))
    mix = mix + g_ref[:, 2 * D_MODEL:].astype(F32) * dot(c_ref[...], wc_ref[...])
    out_ref[...] = x_ref[...] + dot(mix.astype(BF16), wo_ref[...])


def _merge(o, y_tm, c, gates, x2d, wa, w1, w2, wc, wo, seq):
    n = x2d.shape[0]
    tm = ROW_TILE
    nsb = seq // tm
    row = pl.BlockSpec((tm, D_MODEL), lambda i: (i, 0))
    wgt = pl.BlockSpec((D_MODEL, D_MODEL), lambda i: (0, 0))
    return pl.pallas_call(
        _merge_kernel,
        grid=(n // tm,),
        in_specs=[
            row,
            pl.BlockSpec((tm, D_MODEL), lambda i: (i % nsb, i // nsb)),
            row,
            pl.BlockSpec((tm, 3 * D_MODEL), lambda i: (i, 0)),
            row,
            wgt, wgt, wgt, wgt, wgt,
        ],
        out_specs=row,
        out_shape=jax.ShapeDtypeStruct((n, D_MODEL), F32),
        compiler_params=_params(("parallel",)),
        name="merge",
    )(o, y_tm, c, gates, x2d, wa, w1, w2, wc, wo)


def _ffn_kernel(x_ref, xh_ref, g_ref, wu_ref, dw_ref, wd_ref, out_ref,
                h_scr, ua_scr, ub_scr, acc_scr, *, nsb):
    tm = x_ref.shape[0]
    first = (pl.program_id(0) % nsb) == 0
    g = g_ref[...]
    h_scr[0:FFN_HALO, :] = jnp.where(first, 0.0, _rms_rows(xh_ref[...], g)).astype(BF16)
    h_scr[FFN_HALO:, :] = _rms_rows(x_ref[...], g).astype(BF16)
    lead = FFN_HALO - (FFN_KERNEL - 1)

    def taps(scr, dw):
        out = dw[0:1, :] * scr[lead:lead + tm, :]
        for k in range(1, FFN_KERNEL):
            out = out + dw[k:k + 1, :] * scr[lead + k:lead + k + tm, :]
        return out

    for j in range(FFN_TILES):
        lin = slice(j * FFN_TILE, (j + 1) * FFN_TILE)
        gate = slice(FFN_HIDDEN + j * FFN_TILE, FFN_HIDDEN + (j + 1) * FFN_TILE)
        h = h_scr[...]
        ua_scr[...] = jnp.dot(h, wu_ref[:, lin], preferred_element_type=F32)
        ub_scr[...] = jnp.dot(h, wu_ref[:, gate], preferred_element_type=F32)
        act = jax.nn.gelu(taps(ub_scr, dw_ref[:, gate])) * taps(ua_scr, dw_ref[:, lin])
        part = jnp.dot(act.astype(BF16), wd_ref[lin, :], preferred_element_type=F32)
        if j == 0:
            acc_scr[...] = part
        else:
            acc_scr[...] += part
    out_ref[...] = x_ref[...] + acc_scr[...]


def _ffn(x2d, g, wu, dw, wd, seq):
    n = x2d.shape[0]
    tm = ROW_TILE
    nsb = seq // tm
    hb = tm // FFN_HALO
    const = lambda i: (0, 0)
    return pl.pallas_call(
        functools.partial(_ffn_kernel, nsb=nsb),
        grid=(n // tm,),
        in_specs=[
            pl.BlockSpec((tm, D_MODEL), lambda i: (i, 0)),
            pl.BlockSpec((FFN_HALO, D_MODEL), lambda i: (jnp.maximum(i * hb - 1, 0), 0)),
            pl.BlockSpec((1, D_MODEL), lambda i: (0, 0)),
            pl.BlockSpec((D_MODEL, 2 * FFN_HIDDEN), const),
            pl.BlockSpec((FFN_KERNEL, 2 * FFN_HIDDEN), const),
            pl.BlockSpec((FFN_HIDDEN, D_MODEL), const),
        ],
        out_specs=pl.BlockSpec((tm, D_MODEL), lambda i: (i, 0)),
        out_shape=jax.ShapeDtypeStruct((n, D_MODEL), F32),
        scratch_shapes=[
            pltpu.VMEM((tm + FFN_HALO, D_MODEL), BF16),
            pltpu.VMEM((tm + FFN_HALO, FFN_TILE), F32),
            pltpu.VMEM((tm + FFN_HALO, FFN_TILE), F32),
            pltpu.VMEM((tm, D_MODEL), F32),
        ],
        compiler_params=_params(("parallel",)),
        name="ffn",
    )(x2d, x2d, g, wu, dw, wd)


def kernel(x, norm_mix, w_in, qk_gain_q, qk_gain_k, lambda_q1, lambda_k1, lambda_q2, lambda_k2, diff_subln, rel_bias, w_attn_out, s5_lambda_re, s5_lambda_im, s5_log_step, s5_b_re, s5_b_im, s5_c_re, s5_c_im, s5_d, s5_glu_w1, s5_glu_w2, conv_dw_w, conv_dw_b, conv_ln_g, conv_ln_b, conv_w_out, w_out, norm_ffn, ffn_w_up, ffn_dw_w, ffn_w_down):
    batch, seq, _ = x.shape
    assert batch == SUBLANES, "the S5 scan keeps one batch entry per sublane"
    n = batch * seq
    row = lambda p: p.reshape(1, -1).astype(F32)
    bias = _bias_tiles(rel_bias.astype(F32))
    x2d = x.reshape(n, D_MODEL)
    for l in range(DEPTH):
        lam_init = 0.8 - 0.6 * math.exp(-0.3 * l)
        q, k, v, u_tm, c_glu, gates = _in_proj(
            x2d, row(norm_mix[l]), w_in[l].astype(BF16),
            row(jnp.tile(qk_gain_q[l], 2)), row(jnp.tile(qk_gain_k[l], 2)), batch, seq)

        lamv = jnp.stack([lambda_q1[l], lambda_k1[l], lambda_q2[l], lambda_k2[l]]).astype(F32)
        shape3 = (batch, seq, D_MODEL)
        o = _attention(q.reshape(shape3), k.reshape(shape3), v.reshape(shape3), bias, lamv,
                       row(diff_subln[l]), lam_init)

        a, b_blk, c_blk = _s5_discretise(s5_lambda_re[l], s5_lambda_im[l], s5_log_step[l],
                                         s5_b_re[l], s5_b_im[l], s5_c_re[l], s5_c_im[l])
        y_tm = _s5(u_tm.reshape(seq * batch, D_MODEL), a, b_blk, c_blk, row(s5_d[l]), batch, seq)

        c = _conv_module(c_glu, conv_dw_w[l].astype(F32), row(conv_dw_b[l]), row(conv_ln_g[l]),
                         row(conv_ln_b[l]), seq)

        x2d = _merge(o.reshape(n, D_MODEL), y_tm.reshape(seq, batch * D_MODEL), c, gates, x2d,
                     w_attn_out[l].astype(BF16), s5_glu_w1[l].astype(BF16),
                     s5_glu_w2[l].astype(BF16), conv_w_out[l].astype(BF16), w_out[l].astype(BF16),
                     seq)

        x2d = _ffn(x2d, row(norm_ffn[l]), ffn_w_up[l].astype(BF16), ffn_dw_w[l].astype(F32),
                   ffn_w_down[l].astype(BF16), seq)
    return x2d.reshape(batch, seq, D_MODEL)
```

```python
import functools
import math

import jax
import jax.numpy as jnp
from jax import lax
from jax.experimental import pallas as pl
from jax.experimental.pallas import tpu as pltpu

F32 = jnp.float32
BF16 = jnp.bfloat16

D_MODEL = 1024
DEPTH = 2
CHUNK = 64
HEADS = 8
HEAD_DIM = 64
HEAD_WIDTH = 2 * HEAD_DIM
S5_GROUP = 16
S5_GROUPS = 64
S5_STATE = 64
CONV_KERNEL = 31
FFN_HIDDEN = 2816
FFN_KERNEL = 3
REL_BUCKETS = 32
REL_MAX_DIST = 128
EPS = 1e-6
IN_TILES = 9
NEG_INF = -1e30

LANES = 128
SUBLANES = 8
MXU_DIM = 256
VMEM_LIMIT = 56 * 1024 * 1024

ROW_TILE = 512
IN_ROW_TILE = 1024
ATTN_Q = 512
ATTN_K = 256
ATTN_KQ = ATTN_Q // ATTN_K
LOG2E = math.log2(math.e)
S5_STEPS = 32
S5_SLABS = D_MODEL // MXU_DIM
S5_SLAB_STATE = (MXU_DIM // S5_GROUP) * S5_STATE
CONV_HALO = 32
CONV_TILE = 256
FFN_HALO = 16
FFN_TILE = 256
FFN_TILES = FFN_HIDDEN // FFN_TILE


def _params(semantics):
    return pltpu.CompilerParams(dimension_semantics=semantics, vmem_limit_bytes=VMEM_LIMIT)


def _rms_rows(x, g):
    return x * lax.rsqrt(jnp.mean(x * x, axis=-1, keepdims=True) + EPS) * g


def _head_norm(z, gain, scale):
    lane = lax.broadcasted_iota(jnp.int32, (1, HEAD_WIDTH), 1)
    first = lane < HEAD_DIM
    outs = []
    for h in range(z.shape[1] // HEAD_WIDTH):
        slab = z[:, h * HEAD_WIDTH:(h + 1) * HEAD_WIDTH]
        sq = slab * slab
        s0 = jnp.sum(jnp.where(first, sq, 0.0), axis=-1, keepdims=True)
        s1 = jnp.sum(jnp.where(first, 0.0, sq), axis=-1, keepdims=True)
        r = lax.rsqrt(jnp.where(first, s0, s1) * (1.0 / HEAD_DIM) + EPS)
        outs.append((slab * r * gain * scale).astype(BF16))
    return jnp.concatenate(outs, axis=-1)


def _in_proj_kernel(x_ref, g_ref, w_ref, gq_ref, gk_ref,
                    q_ref, k_ref, v_ref, u_ref, c_ref, gate_ref, h_scr, a_scr):
    j = pl.program_id(1)

    @pl.when(j == 0)
    def _():
        h_scr[...] = _rms_rows(x_ref[...], g_ref[...]).astype(BF16)

    def project(epilogue):
        for c in range(D_MODEL // MXU_DIM):
            cols = slice(c * MXU_DIM, (c + 1) * MXU_DIM)
            epilogue(cols, jnp.dot(h_scr[...], w_ref[:, cols], preferred_element_type=F32))

    def store(ref, fn):
        def epilogue(cols, z):
            ref[:, cols] = fn(cols, z)
        return epilogue

    @pl.when(j == 0)
    def _():
        project(store(q_ref, lambda cols, z: _head_norm(z, gq_ref[...], HEAD_DIM ** -0.5 * LOG2E)))

    @pl.when(j == 1)
    def _():
        project(store(k_ref, lambda cols, z: _head_norm(z, gk_ref[...], 1.0)))

    @pl.when(j == 2)
    def _():
        project(store(v_ref, lambda cols, z: z.astype(BF16)))

    @pl.when(j == 3)
    def _():
        project(store(u_ref, lambda cols, z: z))

    @pl.when(j == 4)
    def _():
        project(store(a_scr, lambda cols, z: z))

    @pl.when(j == 5)
    def _():
        project(store(c_ref, lambda cols, z: (a_scr[:, cols] * jax.nn.sigmoid(z)).astype(BF16)))

    @pl.when(j >= 6)
    def _():
        project(store(gate_ref, lambda cols, z: jax.nn.sigmoid(z).astype(BF16)))


def _in_proj(x2d, g, w, gq, gk, batch, seq):
    n = x2d.shape[0]
    tm = IN_ROW_TILE
    nsb = seq // tm
    row = lambda i, j: (i, 0)
    return pl.pallas_call(
        _in_proj_kernel,
        grid=(n // tm, IN_TILES),
        in_specs=[
            pl.BlockSpec((tm, D_MODEL), row),
            pl.BlockSpec((1, D_MODEL), lambda i, j: (0, 0)),
            pl.BlockSpec((D_MODEL, D_MODEL), lambda i, j: (0, j)),
            pl.BlockSpec((1, HEAD_WIDTH), lambda i, j: (0, 0)),
            pl.BlockSpec((1, HEAD_WIDTH), lambda i, j: (0, 0)),
        ],
        out_specs=[
            pl.BlockSpec((tm, D_MODEL), row),
            pl.BlockSpec((tm, D_MODEL), row),
            pl.BlockSpec((tm, D_MODEL), row),
            pl.BlockSpec((tm, D_MODEL), lambda i, j: (i % nsb, i // nsb)),
            pl.BlockSpec((tm, D_MODEL), row),
            pl.BlockSpec((tm, D_MODEL), lambda i, j: (i, jnp.maximum(j - 6, 0))),
        ],
        out_shape=[
            jax.ShapeDtypeStruct((n, D_MODEL), BF16),
            jax.ShapeDtypeStruct((n, D_MODEL), BF16),
            jax.ShapeDtypeStruct((n, D_MODEL), BF16),
            jax.ShapeDtypeStruct((seq, batch * D_MODEL), F32),
            jax.ShapeDtypeStruct((n, D_MODEL), BF16),
            jax.ShapeDtypeStruct((n, 3 * D_MODEL), BF16),
        ],
        scratch_shapes=[pltpu.VMEM((tm, D_MODEL), BF16), pltpu.VMEM((tm, D_MODEL), F32)],
        compiler_params=_params(("parallel", "arbitrary")),
        name="in_proj",
    )(x2d, g, w, gq, gk)


def _t5_bucket_ids(rel):
    nb = REL_BUCKETS // 2
    n = -rel
    ret = jnp.where(n < 0, nb, 0)
    n = jnp.abs(n)
    max_exact = nb // 2
    nf = jnp.maximum(n, 1).astype(F32)
    large = max_exact + (jnp.log(nf / max_exact) / math.log(REL_MAX_DIST / max_exact)
                         * (nb - max_exact)).astype(jnp.int32)
    large = jnp.minimum(large, nb - 1)
    return ret + jnp.where(n < max_exact, n, large)


def _bias_kernel(far_ref, tab_ref, ids_ref, out_ref):
    h = pl.program_id(0)
    tq, tk = ATTN_Q, ATTN_K
    ids = ids_ref[...]
    acc = jnp.zeros(ids.shape, F32)
    for b in range(REL_BUCKETS):
        acc = jnp.where(ids == b, tab_ref[b, h], acc)
    acc = (acc - tab_ref[far_ref[0], h]) * LOG2E
    qi = lax.broadcasted_iota(jnp.int32, (tq, tk), 0)
    kj = lax.broadcasted_iota(jnp.int32, (tq, tk), 1)
    out_ref[0] = jnp.zeros((2 * tq, tk), F32)
    for n, o in enumerate(range(-1, ATTN_KQ)):
        visible = ((o + 1) * tk + kj) // CHUNK <= (tk + qi) // CHUNK
        tile = jnp.where(visible, acc[n], NEG_INF)
        out_ref[n + 1, 0:tq] = tile
        out_ref[n + 1, tq:2 * tq] = tile


def _bias_tiles(rel_bias):
    tq, tk = ATTN_Q, ATTN_K
    qi = jnp.arange(tq, dtype=jnp.int32)[:, None]
    kj = jnp.arange(tk, dtype=jnp.int32)[None, :]
    ids = jnp.stack([_t5_bucket_ids(o * tk + kj - qi) for o in range(-1, ATTN_KQ)])
    far = _t5_bucket_ids(jnp.full((1,), -(tk + 1), jnp.int32))
    n_tiles = ATTN_KQ + 2
    return pl.pallas_call(
        _bias_kernel,
        grid=(HEADS,),
        in_specs=[
            pl.BlockSpec(memory_space=pltpu.SMEM),
            pl.BlockSpec(memory_space=pltpu.SMEM),
            pl.BlockSpec((ATTN_KQ + 1, tq, tk), lambda h: (0, 0, 0)),
        ],
        out_specs=pl.BlockSpec((None, n_tiles, 2 * tq, tk), lambda h: (h, 0, 0, 0)),
        out_shape=jax.ShapeDtypeStruct((HEADS, n_tiles, 2 * tq, tk), F32),
        compiler_params=_params(("parallel",)),
        name="bias_tiles",
    )(far, rel_bias, ids)


def _attn_kernel(lamv_ref, subln_ref, bias_ref, q_ref, k_ref, v_ref, o_ref,
                 s_scr, vx_scr, m_scr, acc_scr, *, lam_init):
    tq, tk = ATTN_Q, ATTN_K
    i = pl.program_id(2)
    n_groups = i + 1

    @pl.when(i == 0)
    def _():
        vx_scr[:, 0:HEAD_WIDTH] = v_ref[...]
        vx_scr[:, HEAD_WIDTH:] = jnp.ones((vx_scr.shape[0], HEAD_WIDTH), BF16)

    q = q_ref[...]
    lane = lax.broadcasted_iota(jnp.int32, (1, HEAD_WIDTH), 1)
    first = lane < HEAD_DIM
    zero = jnp.zeros_like(q)
    qs = jnp.concatenate([jnp.where(first, q, zero), jnp.where(first, zero, q)], axis=0)

    def rows(j):
        return pl.ds(pl.multiple_of(j * tk, tk), tk)

    def score_tile(j):
        s = lax.dot_general(qs, k_ref[rows(j), :], (((1,), (1,)), ((), ())),
                            preferred_element_type=F32)
        s = s + bias_ref[jnp.clip(j - ATTN_KQ * i + 2, 0, ATTN_KQ + 1)]
        s_scr[j] = s
        return jnp.maximum(s[:, 0:LANES], s[:, LANES:2 * LANES])

    m_scr[...] = jnp.full(m_scr.shape, NEG_INF, F32)

    def max_group(g, carry):
        tile_max = [score_tile(g * ATTN_KQ + u) for u in range(ATTN_KQ)]
        m_scr[...] = functools.reduce(jnp.maximum, tile_max, m_scr[...])
        return carry

    lax.fori_loop(0, n_groups, max_group, 0)
    m_scr[...] = jnp.broadcast_to(jnp.max(m_scr[...], axis=-1, keepdims=True), m_scr.shape)
    acc_scr[...] = jnp.zeros(acc_scr.shape, F32)

    def prob_tile(j):
        m_rep = m_scr[...]
        p = jnp.exp2(s_scr[j] - jnp.concatenate([m_rep, m_rep], axis=1))
        return jnp.dot(p.astype(BF16), vx_scr[rows(j), :], preferred_element_type=F32)

    def sum_group(g, carry):
        parts = [prob_tile(g * ATTN_KQ + u) for u in range(ATTN_KQ)]
        acc_scr[...] += functools.reduce(lambda a, b: a + b, parts)
        return carry

    lax.fori_loop(0, n_groups, sum_group, 0)

    lv = lamv_ref[...]
    lam = (jnp.exp(jnp.sum(lv[0:1] * lv[1:2], axis=-1, keepdims=True))
           - jnp.exp(jnp.sum(lv[2:3] * lv[3:4], axis=-1, keepdims=True)) + lam_init)
    pv, l = acc_scr[:, 0:HEAD_WIDTH], acc_scr[:, HEAD_WIDTH:]
    o = pv[0:tq] / l[0:tq] - lam * (pv[tq:2 * tq] / l[tq:2 * tq])
    o_ref[...] = (_rms_rows(o, subln_ref[...]) * (1.0 - lam_init)).astype(BF16)


def _attention(q, k, v, bias, lamv, subln, lam_init):
    b, s, _ = q.shape
    tq, tk = ATTN_Q, ATTN_K
    return pl.pallas_call(
        functools.partial(_attn_kernel, lam_init=lam_init),
        grid=(b, HEADS, s // tq),
        in_specs=[
            pl.BlockSpec((4, HEAD_DIM), lambda bi, h, i: (0, 0)),
            pl.BlockSpec((1, HEAD_WIDTH), lambda bi, h, i: (0, 0)),
            pl.BlockSpec((None, ATTN_KQ + 2, 2 * tq, tk), lambda bi, h, i: (h, 0, 0, 0)),
            pl.BlockSpec((None, tq, HEAD_WIDTH), lambda bi, h, i: (bi, i, h)),
            pl.BlockSpec((None, s, HEAD_WIDTH), lambda bi, h, i: (bi, 0, h)),
            pl.BlockSpec((None, s, HEAD_WIDTH), lambda bi, h, i: (bi, 0, h)),
        ],
        out_specs=pl.BlockSpec((None, tq, HEAD_WIDTH), lambda bi, h, i: (bi, i, h)),
        out_shape=jax.ShapeDtypeStruct((b, s, D_MODEL), BF16),
        scratch_shapes=[
            pltpu.VMEM((s // tk, 2 * tq, tk), F32),
            pltpu.VMEM((s, 2 * HEAD_WIDTH), BF16),
            pltpu.VMEM((2 * tq, LANES), F32),
            pltpu.VMEM((2 * tq, 2 * HEAD_WIDTH), F32),
        ],
        compiler_params=_params(("parallel", "parallel", "arbitrary")),
        name="diff_attention",
    )(lamv, subln, bias, q, k, v)


def _s5_discretise(lam_re, lam_im, log_step, b_re, b_im, c_re, c_im):
    step = jnp.exp(log_step.astype(F32))[:, None]
    lr, li = lam_re.astype(F32), lam_im.astype(F32)
    mag = jnp.exp(lr * step)
    ab_re, ab_im = mag * jnp.cos(li * step), mag * jnp.sin(li * step)
    den = lr * lr + li * li
    nr, ni = ab_re - 1.0, ab_im
    f_re = (nr * lr + ni * li) / den
    f_im = (ni * lr - nr * li) / den
    br, bi = b_re.astype(F32), b_im.astype(F32)
    bb_re = f_re[..., None] * br - f_im[..., None] * bi
    bb_im = f_re[..., None] * bi + f_im[..., None] * br
    gs = MXU_DIM // S5_GROUP
    state = jnp.arange(S5_SLAB_STATE, dtype=jnp.int32)
    chan = jnp.arange(MXU_DIM, dtype=jnp.int32)
    spread = (state[None, :] % S5_STATE == jnp.arange(S5_STATE, dtype=jnp.int32)[:, None]).astype(BF16)
    own = (chan[:, None] // S5_GROUP) == (state[None, :] // S5_STATE)

    def in_blocks(bb):
        compact = bb.reshape(S5_SLABS, gs, S5_STATE, S5_GROUP).transpose(0, 1, 3, 2)
        compact = compact.reshape(S5_SLABS, MXU_DIM, S5_STATE).astype(BF16)
        blk = jnp.einsum('krp,pc->krc', compact, spread, preferred_element_type=F32)
        return jnp.where(own, blk, 0.0).astype(BF16)

    def out_blocks(c):
        compact = c.reshape(S5_SLABS, MXU_DIM, S5_STATE).astype(BF16)
        blk = jnp.einsum('pc,krp->kcr', spread, compact, preferred_element_type=F32)
        return jnp.where(own.T, blk, 0.0).astype(BF16)

    b_blk = jnp.concatenate([in_blocks(bb_re), in_blocks(bb_im)], axis=2)
    c_blk = jnp.concatenate([out_blocks(c_re.astype(F32)), out_blocks(-c_im.astype(F32))], axis=1)
    a = jnp.stack([ab_re.reshape(S5_SLABS, S5_SLAB_STATE), ab_im.reshape(S5_SLABS, S5_SLAB_STATE)],
                  axis=1)
    return a, b_blk, c_blk


def _s5_kernel(u_ref, a_ref, b_ref, c_ref, d_ref, y_ref, x_scr, st_scr, *, batch):
    w = S5_SLAB_STATE

    @pl.when(pl.program_id(0) == 0)
    def _():
        st_scr[...] = jnp.zeros(st_scr.shape, F32)

    u = u_ref[...]
    ub = u.astype(BF16)
    for kt in range(S5_SLABS):
        x_scr[:, kt * 2 * w:(kt + 1) * 2 * w] = jnp.dot(
            ub[:, kt * MXU_DIM:(kt + 1) * MXU_DIM], b_ref[kt], preferred_element_type=F32)

    for kt in range(S5_SLABS):
        re = slice(kt * 2 * w, kt * 2 * w + w)
        im = slice(kt * 2 * w + w, (kt + 1) * 2 * w)
        ar = jnp.broadcast_to(a_ref[kt, 0:1, :], (batch, w))
        ai = jnp.broadcast_to(a_ref[kt, 1:2, :], (batch, w))

        def step(t, carry, re=re, im=im, ar=ar, ai=ai):
            xr, xi = carry
            rows = pl.ds(pl.multiple_of(t * batch, batch), batch)
            nxr = ar * xr - ai * xi + x_scr[rows, re]
            nxi = ar * xi + ai * xr + x_scr[rows, im]
            x_scr[rows, re] = nxr
            x_scr[rows, im] = nxi
            return nxr, nxi

        xr, xi = lax.fori_loop(0, S5_STEPS, step, (st_scr[:, re], st_scr[:, im]), unroll=4)
        st_scr[:, re] = xr
        st_scr[:, im] = xi

    for kt in range(S5_SLABS):
        cols = slice(kt * MXU_DIM, (kt + 1) * MXU_DIM)
        y = jnp.dot(x_scr[:, kt * 2 * w:(kt + 1) * 2 * w].astype(BF16), c_ref[kt],
                    preferred_element_type=F32)
        y = y + d_ref[:, cols] * u[:, cols]
        y_ref[:, cols] = jax.nn.gelu(y).astype(BF16)


def _s5(u_tm, a, b_blk, c_blk, d, batch, seq):
    rows = S5_STEPS * batch
    w = S5_SLAB_STATE
    return pl.pallas_call(
        functools.partial(_s5_kernel, batch=batch),
        grid=(seq // S5_STEPS,),
        in_specs=[
            pl.BlockSpec((rows, D_MODEL), lambda c: (c, 0)),
            pl.BlockSpec((S5_SLABS, 2, w), lambda c: (0, 0, 0)),
            pl.BlockSpec((S5_SLABS, MXU_DIM, 2 * w), lambda c: (0, 0, 0)),
            pl.BlockSpec((S5_SLABS, 2 * w, MXU_DIM), lambda c: (0, 0, 0)),
            pl.BlockSpec((1, D_MODEL), lambda c: (0, 0)),
        ],
        out_specs=pl.BlockSpec((rows, D_MODEL), lambda c: (c, 0)),
        out_shape=jax.ShapeDtypeStruct((seq * batch, D_MODEL), BF16),
        scratch_shapes=[
            pltpu.VMEM((rows, S5_SLABS * 2 * w), F32),
            pltpu.VMEM((batch, S5_SLABS * 2 * w), F32),
        ],
        compiler_params=_params(("arbitrary",)),
        name="s5_scan",
    )(u_tm, a, b_blk, c_blk, d)


def _conv_kernel(c_ref, halo_ref, w_ref, b_ref, lg_ref, lb_ref, o_ref,
                 xe_scr, xs_scr, wb_scr, y_scr, *, nsb):
    tm = c_ref.shape[0]
    strips = D_MODEL // LANES
    first = (pl.program_id(0) % nsb) == 0
    xe_scr[0:CONV_HALO, :] = jnp.where(first, 0.0, halo_ref[...].astype(F32))
    xe_scr[CONV_HALO:, :] = c_ref[...].astype(F32)
    for b in range(SUBLANES):
        span = tm + CONV_HALO - (SUBLANES if b else 0)
        for cb in range(strips):
            xs_scr[b, cb, 0:span, :] = xe_scr[b:b + span, cb * LANES:(cb + 1) * LANES]
    for cb in range(strips):
        cols = slice(cb * LANES, (cb + 1) * LANES)
        for k in range(CONV_KERNEL):
            wb_scr[k, cb] = jnp.broadcast_to(w_ref[k:k + 1, cols], (SUBLANES, LANES))
        wb_scr[CONV_KERNEL, cb] = jnp.broadcast_to(b_ref[:, cols], (SUBLANES, LANES))
    lead = CONV_HALO - (CONV_KERNEL - 1)
    reps = tm // SUBLANES

    def strip(cb, carry):
        acc = jnp.tile(wb_scr[CONV_KERNEL, cb], (reps, 1))
        for k in range(CONV_KERNEL):
            shift = (lead + k) % SUBLANES
            base = lead + k - shift
            acc = acc + jnp.tile(wb_scr[k, cb], (reps, 1)) * xs_scr[shift, cb, base:base + tm, :]
        y_scr[cb] = acc
        return carry

    lax.fori_loop(0, strips, strip, 0)
    y = jnp.concatenate([y_scr[cb] for cb in range(strips)], axis=1)
    yc = y - jnp.mean(y, axis=-1, keepdims=True)
    yn = yc * lax.rsqrt(jnp.mean(yc * yc, axis=-1, keepdims=True) + EPS) * lg_ref[...] + lb_ref[...]
    o_ref[...] = (yn * jax.nn.sigmoid(yn)).astype(BF16)


def _conv_module(c, w, b, lg, lb, seq):
    n = c.shape[0]
    tm = CONV_TILE
    nsb = seq // tm
    hb = tm // CONV_HALO
    vec = pl.BlockSpec((1, D_MODEL), lambda i: (0, 0))
    return pl.pallas_call(
        functools.partial(_conv_kernel, nsb=nsb),
        grid=(n // tm,),
        in_specs=[
            pl.BlockSpec((tm, D_MODEL), lambda i: (i, 0)),
            pl.BlockSpec((CONV_HALO, D_MODEL), lambda i: (jnp.maximum(i * hb - 1, 0), 0)),
            pl.BlockSpec((CONV_KERNEL, D_MODEL), lambda i: (0, 0)),
            vec, vec, vec,
        ],
        out_specs=pl.BlockSpec((tm, D_MODEL), lambda i: (i, 0)),
        out_shape=jax.ShapeDtypeStruct((n, D_MODEL), BF16),
        scratch_shapes=[pltpu.VMEM((tm + CONV_HALO, D_MODEL), F32),
                        pltpu.VMEM((SUBLANES, D_MODEL // LANES, tm + CONV_HALO, LANES), F32),
                        pltpu.VMEM((CONV_KERNEL + 1, D_MODEL // LANES, SUBLANES, LANES), F32),
                        pltpu.VMEM((D_MODEL // LANES, tm, LANES), F32)],
        compiler_params=_params(("parallel",)),
        name="conv_module",
    )(c, c, w, b, lg, lb)


def _merge_kernel(o_ref, y_ref, c_ref, g_ref, x_ref, wa_ref, w1_ref, w2_ref, wc_ref, wo_ref, out_ref):
    dot = functools.partial(jnp.dot, preferred_element_type=F32)
    y = y_ref[...]
    mix = g_ref[:, 0:D_MODEL].astype(F32) * dot(o_ref[...], wa_ref[...])
    mix = mix + g_ref[:, D_MODEL:2 * D_MODEL].astype(F32) * (
        dot(y, w1_ref[...]) * jax.nn.sigmoid(dot(y, w2_ref[...])))
    mix = mix + g_ref[:, 2 * D_MODEL:].astype(F32) * dot(c_ref[...], wc_ref[...])
    out_ref[...] = x_ref[...] + dot(mix.astype(BF16), wo_ref[...])


def _merge(o, y_tm, c, gates, x2d, wa, w1, w2, wc, wo, seq):
    n = x2d.shape[0]
    tm = ROW_TILE
    nsb = seq // tm
    row = pl.BlockSpec((tm, D_MODEL), lambda i: (i, 0))
    wgt = pl.BlockSpec((D_MODEL, D_MODEL), lambda i: (0, 0))
    return pl.pallas_call(
        _merge_kernel,
        grid=(n // tm,),
        in_specs=[
            row,
            pl.BlockSpec((tm, D_MODEL), lambda i: (i % nsb, i // nsb)),
            row,
            pl.BlockSpec((tm, 3 * D_MODEL), lambda i: (i, 0)),
            row,
            wgt, wgt, wgt, wgt, wgt,
        ],
        out_specs=row,
        out_shape=jax.ShapeDtypeStruct((n, D_MODEL), F32),
        compiler_params=_params(("parallel",)),
        name="merge",
    )(o, y_tm, c, gates, x2d, wa, w1, w2, wc, wo)


def _ffn_kernel(x_ref, xh_ref, g_ref, wu_ref, dw_ref, wd_ref, out_ref,
                h_scr, ua_scr, ub_scr, acc_scr, *, nsb):
    tm = x_ref.shape[0]
    first = (pl.program_id(0) % nsb) == 0
    g = g_ref[...]
    h_scr[0:FFN_HALO, :] = jnp.where(first, 0.0, _rms_rows(xh_ref[...], g)).astype(BF16)
    h_scr[FFN_HALO:, :] = _rms_rows(x_ref[...], g).astype(BF16)
    lead = FFN_HALO - (FFN_KERNEL - 1)

    def taps(scr, dw):
        out = dw[0:1, :] * scr[lead:lead + tm, :]
        for k in range(1, FFN_KERNEL):
            out = out + dw[k:k + 1, :] * scr[lead + k:lead + k + tm, :]
        return out

    for j in range(FFN_TILES):
        lin = slice(j * FFN_TILE, (j + 1) * FFN_TILE)
        gate = slice(FFN_HIDDEN + j * FFN_TILE, FFN_HIDDEN + (j + 1) * FFN_TILE)
        h = h_scr[...]
        ua_scr[...] = jnp.dot(h, wu_ref[:, lin], preferred_element_type=F32)
        ub_scr[...] = jnp.dot(h, wu_ref[:, gate], preferred_element_type=F32)
        act = jax.nn.gelu(taps(ub_scr, dw_ref[:, gate])) * taps(ua_scr, dw_ref[:, lin])
        part = jnp.dot(act.astype(BF16), wd_ref[lin, :], preferred_element_type=F32)
        if j == 0:
            acc_scr[...] = part
        else:
            acc_scr[...] += part
    out_ref[...] = x_ref[...] + acc_scr[...]


def _ffn(x2d, g, wu, dw, wd, seq):
    n = x2d.shape[0]
    tm = ROW_TILE
    nsb = seq // tm
    hb = tm // FFN_HALO
    const = lambda i: (0, 0)
    return pl.pallas_call(
        functools.partial(_ffn_kernel, nsb=nsb),
        grid=(n // tm,),
        in_specs=[
            pl.BlockSpec((tm, D_MODEL), lambda i: (i, 0)),
            pl.BlockSpec((FFN_HALO, D_MODEL), lambda i: (jnp.maximum(i * hb - 1, 0), 0)),
            pl.BlockSpec((1, D_MODEL), lambda i: (0, 0)),
            pl.BlockSpec((D_MODEL, 2 * FFN_HIDDEN), const),
            pl.BlockSpec((FFN_KERNEL, 2 * FFN_HIDDEN), const),
            pl.BlockSpec((FFN_HIDDEN, D_MODEL), const),
        ],
        out_specs=pl.BlockSpec((tm, D_MODEL), lambda i: (i, 0)),
        out_shape=jax.ShapeDtypeStruct((n, D_MODEL), F32),
        scratch_shapes=[
            pltpu.VMEM((tm + FFN_HALO, D_MODEL), BF16),
            pltpu.VMEM((tm + FFN_HALO, FFN_TILE), F32),
            pltpu.VMEM((tm + FFN_HALO, FFN_TILE), F32),
            pltpu.VMEM((tm, D_MODEL), F32),
        ],
        compiler_params=_params(("parallel",)),
        name="ffn",
    )(x2d, x2d, g, wu, dw, wd)


def kernel(x, norm_mix, w_in, qk_gain_q, qk_gain_k, lambda_q1, lambda_k1, lambda_q2, lambda_k2, diff_subln, rel_bias, w_attn_out, s5_lambda_re, s5_lambda_im, s5_log_step, s5_b_re, s5_b_im, s5_c_re, s5_c_im, s5_d, s5_glu_w1, s5_glu_w2, conv_dw_w, conv_dw_b, conv_ln_g, conv_ln_b, conv_w_out, w_out, norm_ffn, ffn_w_up, ffn_dw_w, ffn_w_down):
    batch, seq, _ = x.shape
    assert batch == SUBLANES, "the S5 scan keeps one batch entry per sublane"
    n = batch * seq
    row = lambda p: p.reshape(1, -1).astype(F32)
    bias = _bias_tiles(rel_bias.astype(F32))
    x2d = x.reshape(n, D_MODEL)
    for l in range(DEPTH):
        lam_init = 0.8 - 0.6 * math.exp(-0.3 * l)
        q, k, v, u_tm, c_glu, gates = _in_proj(
            x2d, row(norm_mix[l]), w_in[l].astype(BF16),
            row(jnp.tile(qk_gain_q[l], 2)), row(jnp.tile(qk_gain_k[l], 2)), batch, seq)

        lamv = jnp.stack([lambda_q1[l], lambda_k1[l], lambda_q2[l], lambda_k2[l]]).astype(F32)
        shape3 = (batch, seq, D_MODEL)
        o = _attention(q.reshape(shape3), k.reshape(shape3), v.reshape(shape3), bias, lamv,
                       row(diff_subln[l]), lam_init)

        a, b_blk, c_blk = _s5_discretise(s5_lambda_re[l], s5_lambda_im[l], s5_log_step[l],
                                         s5_b_re[l], s5_b_im[l], s5_c_re[l], s5_c_im[l])
        y_tm = _s5(u_tm.reshape(seq * batch, D_MODEL), a, b_blk, c_blk, row(s5_d[l]), batch, seq)

        c = _conv_module(c_glu, conv_dw_w[l].astype(F32), row(conv_dw_b[l]), row(conv_ln_g[l]),
                         row(conv_ln_b[l]), seq)

        x2d = _merge(o.reshape(n, D_MODEL), y_tm.reshape(seq, batch * D_MODEL), c, gates, x2d,
                     w_attn_out[l].astype(BF16), s5_glu_w1[l].astype(BF16),
                     s5_glu_w2[l].astype(BF16), conv_w_out[l].astype(BF16), w_out[l].astype(BF16),
                     seq)

        x2d = _ffn(x2d, row(norm_ffn[l]), ffn_w_up[l].astype(BF16), ffn_dw_w[l].astype(F32),
                   ffn_w_down[l].astype(BF16), seq)
    return x2d.reshape(batch, seq, D_MODEL)
```

```python
import functools
import math

import jax
import jax.numpy as jnp
from jax import lax
from jax.experimental import pallas as pl
from jax.experimental.pallas import tpu as pltpu

F32 = jnp.float32
BF16 = jnp.bfloat16

D_MODEL = 1024
DEPTH = 2
CHUNK = 64
HEADS = 8
HEAD_DIM = 64
HEAD_WIDTH = 2 * HEAD_DIM
S5_GROUP = 16
S5_GROUPS = 64
S5_STATE = 64
CONV_KERNEL = 31
FFN_HIDDEN = 2816
FFN_KERNEL = 3
REL_BUCKETS = 32
REL_MAX_DIST = 128
EPS = 1e-6
IN_TILES = 9
NEG_INF = -1e30

LANES = 128
SUBLANES = 8
MXU_DIM = 256
VMEM_LIMIT = 56 * 1024 * 1024

ROW_TILE = 512
IN_SEQ_TILE = 128
ATTN_Q = 512
ATTN_K = 256
ATTN_KQ = ATTN_Q // ATTN_K
ATTN_GROUPS = 2
LOG2E = math.log2(math.e)
S5_STEPS = 32
S5_SLABS = D_MODEL // MXU_DIM
S5_SLAB_STATE = (MXU_DIM // S5_GROUP) * S5_STATE
CONV_HALO = 32
CONV_TILE = 256
FFN_HALO = 16
FFN_TILE = 256
FFN_TILES = FFN_HIDDEN // FFN_TILE


def _params(semantics):
    return pltpu.CompilerParams(dimension_semantics=semantics, vmem_limit_bytes=VMEM_LIMIT)


def _sigmoid(x):
    return 0.5 * jnp.tanh(0.5 * x) + 0.5


def _rms_rows(x, g):
    return x * lax.rsqrt(jnp.mean(x * x, axis=-1, keepdims=True) + EPS) * g


def _head_norm(z, gain, scale):
    lane = lax.broadcasted_iota(jnp.int32, (1, HEAD_WIDTH), 1)
    first = lane < HEAD_DIM
    outs = []
    for h in range(z.shape[1] // HEAD_WIDTH):
        slab = z[:, h * HEAD_WIDTH:(h + 1) * HEAD_WIDTH]
        sq = slab * slab
        s0 = jnp.sum(jnp.where(first, sq, 0.0), axis=-1, keepdims=True)
        s1 = jnp.sum(jnp.where(first, 0.0, sq), axis=-1, keepdims=True)
        r = lax.rsqrt(jnp.where(first, s0, s1) * (1.0 / HEAD_DIM) + EPS)
        outs.append((slab * r * gain * scale).astype(BF16))
    return jnp.concatenate(outs, axis=-1)


def _in_proj_kernel(x_ref, g_ref, w_ref, gq_ref, gk_ref,
                    q_ref, k_ref, v_ref, u_ref, c_ref, gate_ref, h_scr, a_scr):
    j = pl.program_id(1)
    batch, ts, _ = x_ref.shape
    rows = batch * ts

    @pl.when(j == 0)
    def _():
        x = x_ref[...].reshape(rows, D_MODEL)
        h_scr[...] = _rms_rows(x, g_ref[...]).astype(BF16)

    def project(epilogue):
        for c in range(D_MODEL // MXU_DIM):
            cols = slice(c * MXU_DIM, (c + 1) * MXU_DIM)
            epilogue(cols, jnp.dot(h_scr[...], w_ref[:, cols], preferred_element_type=F32))

    def store(ref, fn):
        def epilogue(cols, z):
            val = fn(cols, z)
            ref[:, :, cols] = val.reshape(batch, ts, MXU_DIM)
        return epilogue

    @pl.when(j == 0)
    def _():
        project(store(q_ref, lambda cols, z: _head_norm(z, gq_ref[...], HEAD_DIM ** -0.5 * LOG2E)))

    @pl.when(j == 1)
    def _():
        project(store(k_ref, lambda cols, z: _head_norm(z, gk_ref[...], 1.0)))

    @pl.when(j == 2)
    def _():
        project(store(v_ref, lambda cols, z: z.astype(BF16)))

    def to_strips(cols, z):
        for s in range(MXU_DIM // LANES):
            a_scr[cols.start // LANES + s] = z[:, s * LANES:(s + 1) * LANES]

    def from_strips(cols):
        first = cols.start // LANES
        return jnp.concatenate([a_scr[first + s] for s in range(MXU_DIM // LANES)], axis=1)

    @pl.when(j == 3)
    def _():
        project(to_strips)

        def to_time_major(t, carry):
            u_ref[t] = jnp.concatenate([a_scr[s, pl.ds(t, batch, stride=ts), :]
                                        for s in range(D_MODEL // LANES)], axis=1)
            return carry

        lax.fori_loop(0, ts, to_time_major, 0, unroll=16)

    @pl.when(j == 4)
    def _():
        project(to_strips)

    @pl.when(j == 5)
    def _():
        project(store(c_ref, lambda cols, z: (from_strips(cols) * _sigmoid(z)).astype(BF16)))

    @pl.when(j >= 6)
    def _():
        project(store(gate_ref, lambda cols, z: _sigmoid(z).astype(BF16)))


def _in_proj(x, g, w, gq, gk):
    batch, seq, _ = x.shape
    ts = IN_SEQ_TILE
    rows = batch * ts
    blk = (batch, ts, D_MODEL)
    tile = lambda i, j: (0, i, 0)
    act = jax.ShapeDtypeStruct((batch, seq, D_MODEL), BF16)
    return pl.pallas_call(
        _in_proj_kernel,
        grid=(seq // ts, IN_TILES),
        in_specs=[
            pl.BlockSpec(blk, tile),
            pl.BlockSpec((1, D_MODEL), lambda i, j: (0, 0)),
            pl.BlockSpec((D_MODEL, D_MODEL), lambda i, j: (0, j)),
            pl.BlockSpec((1, HEAD_WIDTH), lambda i, j: (0, 0)),
            pl.BlockSpec((1, HEAD_WIDTH), lambda i, j: (0, 0)),
        ],
        out_specs=[
            pl.BlockSpec(blk, tile),
            pl.BlockSpec(blk, tile),
            pl.BlockSpec(blk, tile),
            pl.BlockSpec((ts, batch, D_MODEL), lambda i, j: (i, 0, 0)),
            pl.BlockSpec(blk, tile),
            pl.BlockSpec(blk, lambda i, j: (0, i, jnp.maximum(j - 6, 0))),
        ],
        out_shape=[
            act, act, act,
            jax.ShapeDtypeStruct((seq, batch, D_MODEL), F32),
            act,
            jax.ShapeDtypeStruct((batch, seq, 3 * D_MODEL), BF16),
        ],
        scratch_shapes=[pltpu.VMEM((rows, D_MODEL), BF16),
                        pltpu.VMEM((D_MODEL // LANES, rows, LANES), F32)],
        compiler_params=_params(("parallel", "arbitrary")),
        name="in_proj",
    )(x, g, w, gq, gk)


def _t5_bucket_ids(rel):
    nb = REL_BUCKETS // 2
    n = -rel
    ret = jnp.where(n < 0, nb, 0)
    n = jnp.abs(n)
    max_exact = nb // 2
    nf = jnp.maximum(n, 1).astype(F32)
    large = max_exact + (jnp.log(nf / max_exact) / math.log(REL_MAX_DIST / max_exact)
                         * (nb - max_exact)).astype(jnp.int32)
    large = jnp.minimum(large, nb - 1)
    return ret + jnp.where(n < max_exact, n, large)


def _bias_kernel(far_ref, tab_ref, ids_ref, out_ref):
    h = pl.program_id(0)
    tq, tk = ATTN_Q, ATTN_K
    ids = ids_ref[...]
    acc = jnp.zeros(ids.shape, F32)
    for b in range(REL_BUCKETS):
        acc = jnp.where(ids == b, tab_ref[b, h], acc)
    acc = (acc - tab_ref[far_ref[0], h]) * LOG2E
    qi = lax.broadcasted_iota(jnp.int32, (tq, tk), 0)
    kj = lax.broadcasted_iota(jnp.int32, (tq, tk), 1)
    out_ref[0] = jnp.zeros((2 * tq, tk), F32)
    for n, o in enumerate(range(-1, ATTN_KQ)):
        visible = ((o + 1) * tk + kj) // CHUNK <= (tk + qi) // CHUNK
        tile = jnp.where(visible, acc[n], NEG_INF)
        out_ref[n + 1, 0:tq] = tile
        out_ref[n + 1, tq:2 * tq] = tile


def _bias_tiles(rel_bias):
    tq, tk = ATTN_Q, ATTN_K
    qi = jnp.arange(tq, dtype=jnp.int32)[:, None]
    kj = jnp.arange(tk, dtype=jnp.int32)[None, :]
    ids = jnp.stack([_t5_bucket_ids(o * tk + kj - qi) for o in range(-1, ATTN_KQ)])
    far = _t5_bucket_ids(jnp.full((1,), -(tk + 1), jnp.int32))
    n_tiles = ATTN_KQ + 2
    return pl.pallas_call(
        _bias_kernel,
        grid=(HEADS,),
        in_specs=[
            pl.BlockSpec(memory_space=pltpu.SMEM),
            pl.BlockSpec(memory_space=pltpu.SMEM),
            pl.BlockSpec((ATTN_KQ + 1, tq, tk), lambda h: (0, 0, 0)),
        ],
        out_specs=pl.BlockSpec((None, n_tiles, 2 * tq, tk), lambda h: (h, 0, 0, 0)),
        out_shape=jax.ShapeDtypeStruct((HEADS, n_tiles, 2 * tq, tk), F32),
        compiler_params=_params(("parallel",)),
        name="bias_tiles",
    )(far, rel_bias, ids)


def _attn_kernel(lamv_ref, subln_ref, bias_ref, q_ref, k_ref, v_ref, o_ref,
                 s_scr, vx_scr, m_scr, acc_scr, *, lam_init):
    tq, tk = ATTN_Q, ATTN_K
    i = pl.program_id(2)
    n_groups = i + 1

    @pl.when(i == 0)
    def _():
        vx_scr[:, 0:HEAD_WIDTH] = v_ref[...]
        vx_scr[:, HEAD_WIDTH:] = jnp.ones((vx_scr.shape[0], HEAD_WIDTH), BF16)

    q = q_ref[...]
    lane = lax.broadcasted_iota(jnp.int32, (1, HEAD_WIDTH), 1)
    first = lane < HEAD_DIM
    zero = jnp.zeros_like(q)
    qs = jnp.concatenate([jnp.where(first, q, zero), jnp.where(first, zero, q)], axis=0)

    def rows(j):
        return pl.ds(pl.multiple_of(j * tk, tk), tk)

    def score_tile(j):
        s = lax.dot_general(qs, k_ref[rows(j), :], (((1,), (1,)), ((), ())),
                            preferred_element_type=F32)
        s = s + bias_ref[jnp.clip(j - ATTN_KQ * i + 2, 0, ATTN_KQ + 1)]
        s_scr[j] = s
        return jnp.maximum(s[:, 0:LANES], s[:, LANES:2 * LANES])

    m_scr[...] = jnp.full(m_scr.shape, NEG_INF, F32)

    def sweep(groups_fn):
        def body(p, carry):
            groups_fn(p * ATTN_GROUPS, ATTN_GROUPS)
            return carry

        def tail(g, carry):
            groups_fn(g, 1)
            return carry

        n_full = n_groups // ATTN_GROUPS
        lax.fori_loop(0, n_full, body, 0)
        lax.fori_loop(n_full * ATTN_GROUPS, n_groups, tail, 0)

    def max_groups(g0, count):
        tile_max = [score_tile(g0 * ATTN_KQ + u) for u in range(count * ATTN_KQ)]
        m_scr[...] = functools.reduce(jnp.maximum, tile_max, m_scr[...])

    sweep(max_groups)
    m_scr[...] = jnp.broadcast_to(jnp.max(m_scr[...], axis=-1, keepdims=True), m_scr.shape)
    acc_scr[...] = jnp.zeros(acc_scr.shape, F32)

    def prob_tile(j):
        m_rep = m_scr[...]
        p = jnp.exp2(s_scr[j] - jnp.concatenate([m_rep, m_rep], axis=1))
        return jnp.dot(p.astype(BF16), vx_scr[rows(j), :], preferred_element_type=F32)

    def sum_groups(g0, count):
        parts = [prob_tile(g0 * ATTN_KQ + u) for u in range(count * ATTN_KQ)]
        acc_scr[...] += functools.reduce(lambda a, b: a + b, parts)

    sweep(sum_groups)

    lv = lamv_ref[...]
    lam = (jnp.exp(jnp.sum(lv[0:1] * lv[1:2], axis=-1, keepdims=True))
           - jnp.exp(jnp.sum(lv[2:3] * lv[3:4], axis=-1, keepdims=True)) + lam_init)
    pv, l = acc_scr[:, 0:HEAD_WIDTH], acc_scr[:, HEAD_WIDTH:]
    o = pv[0:tq] / l[0:tq] - lam * (pv[tq:2 * tq] / l[tq:2 * tq])
    o_ref[...] = (_rms_rows(o, subln_ref[...]) * (1.0 - lam_init)).astype(BF16)


def _attention(q, k, v, bias, lamv, subln, lam_init):
    b, s, _ = q.shape
    tq, tk = ATTN_Q, ATTN_K
    return pl.pallas_call(
        functools.partial(_attn_kernel, lam_init=lam_init),
        grid=(b, HEADS, s // tq),
        in_specs=[
            pl.BlockSpec((4, HEAD_DIM), lambda bi, h, i: (0, 0)),
            pl.BlockSpec((1, HEAD_WIDTH), lambda bi, h, i: (0, 0)),
            pl.BlockSpec((None, ATTN_KQ + 2, 2 * tq, tk), lambda bi, h, i: (h, 0, 0, 0)),
            pl.BlockSpec((None, tq, HEAD_WIDTH), lambda bi, h, i: (bi, i, h)),
            pl.BlockSpec((None, s, HEAD_WIDTH), lambda bi, h, i: (bi, 0, h)),
            pl.BlockSpec((None, s, HEAD_WIDTH), lambda bi, h, i: (bi, 0, h)),
        ],
        out_specs=pl.BlockSpec((None, tq, HEAD_WIDTH), lambda bi, h, i: (bi, i, h)),
        out_shape=jax.ShapeDtypeStruct((b, s, D_MODEL), BF16),
        scratch_shapes=[
            pltpu.VMEM((s // tk, 2 * tq, tk), F32),
            pltpu.VMEM((s, 2 * HEAD_WIDTH), BF16),
            pltpu.VMEM((2 * tq, LANES), F32),
            pltpu.VMEM((2 * tq, 2 * HEAD_WIDTH), F32),
        ],
        compiler_params=_params(("parallel", "parallel", "arbitrary")),
        name="diff_attention",
    )(lamv, subln, bias, q, k, v)


def _s5_discretise(lam_re, lam_im, log_step, b_re, b_im, c_re, c_im):
    step = jnp.exp(log_step.astype(F32))[:, None]
    lr, li = lam_re.astype(F32), lam_im.astype(F32)
    mag = jnp.exp(lr * step)
    ab_re, ab_im = mag * jnp.cos(li * step), mag * jnp.sin(li * step)
    den = lr * lr + li * li
    nr, ni = ab_re - 1.0, ab_im
    f_re = (nr * lr + ni * li) / den
    f_im = (ni * lr - nr * li) / den
    br, bi = b_re.astype(F32), b_im.astype(F32)
    bb_re = f_re[..., None] * br - f_im[..., None] * bi
    bb_im = f_re[..., None] * bi + f_im[..., None] * br
    gs = MXU_DIM // S5_GROUP
    state = jnp.arange(S5_SLAB_STATE, dtype=jnp.int32)
    chan = jnp.arange(MXU_DIM, dtype=jnp.int32)
    spread = (state[None, :] % S5_STATE == jnp.arange(S5_STATE, dtype=jnp.int32)[:, None]).astype(BF16)
    own = (chan[:, None] // S5_GROUP) == (state[None, :] // S5_STATE)

    def in_blocks(bb):
        compact = bb.reshape(S5_SLABS, gs, S5_STATE, S5_GROUP).transpose(0, 1, 3, 2)
        compact = compact.reshape(S5_SLABS, MXU_DIM, S5_STATE).astype(BF16)
        blk = jnp.einsum('krp,pc->krc', compact, spread, preferred_element_type=F32)
        return jnp.where(own, blk, 0.0).astype(BF16)

    def out_blocks(c):
        compact = c.reshape(S5_SLABS, MXU_DIM, S5_STATE).astype(BF16)
        blk = jnp.einsum('pc,krp->kcr', spread, compact, preferred_element_type=F32)
        return jnp.where(own.T, blk, 0.0).astype(BF16)

    b_blk = jnp.concatenate([in_blocks(bb_re), in_blocks(bb_im)], axis=2)
    c_blk = jnp.concatenate([out_blocks(c_re.astype(F32)), out_blocks(-c_im.astype(F32))], axis=1)
    a = jnp.stack([ab_re.reshape(S5_SLABS, S5_SLAB_STATE), ab_im.reshape(S5_SLABS, S5_SLAB_STATE)],
                  axis=1)
    return a, b_blk, c_blk


def _s5_kernel(u_ref, a_ref, b_ref, c_ref, d_ref, y_ref, x_scr, st_scr, y_scr, *, batch):
    w = S5_SLAB_STATE

    @pl.when(pl.program_id(0) == 0)
    def _():
        st_scr[...] = jnp.zeros(st_scr.shape, F32)

    u = u_ref[...]
    ub = u.astype(BF16)
    for kt in range(S5_SLABS):
        x_scr[:, kt * 2 * w:(kt + 1) * 2 * w] = jnp.dot(
            ub[:, kt * MXU_DIM:(kt + 1) * MXU_DIM], b_ref[kt], preferred_element_type=F32)

    for kt in range(S5_SLABS):
        re = slice(kt * 2 * w, kt * 2 * w + w)
        im = slice(kt * 2 * w + w, (kt + 1) * 2 * w)
        ar = jnp.broadcast_to(a_ref[kt, 0:1, :], (batch, w))
        ai = jnp.broadcast_to(a_ref[kt, 1:2, :], (batch, w))

        def step(t, carry, re=re, im=im, ar=ar, ai=ai):
            xr, xi = carry
            rows = pl.ds(pl.multiple_of(t * batch, batch), batch)
            nxr = ar * xr - ai * xi + x_scr[rows, re]
            nxi = ar * xi + ai * xr + x_scr[rows, im]
            x_scr[rows, re] = nxr
            x_scr[rows, im] = nxi
            return nxr, nxi

        xr, xi = lax.fori_loop(0, S5_STEPS, step, (st_scr[:, re], st_scr[:, im]), unroll=4)
        st_scr[:, re] = xr
        st_scr[:, im] = xi

    for kt in range(S5_SLABS):
        cols = slice(kt * MXU_DIM, (kt + 1) * MXU_DIM)
        y = jnp.dot(x_scr[:, kt * 2 * w:(kt + 1) * 2 * w].astype(BF16), c_ref[kt],
                    preferred_element_type=F32)
        y = y + d_ref[:, cols] * u[:, cols]
        y = jax.nn.gelu(y)
        for s in range(MXU_DIM // LANES):
            y_scr[kt * (MXU_DIM // LANES) + s] = y[:, s * LANES:(s + 1) * LANES]
    for b in range(batch):
        y_ref[b] = jnp.concatenate([y_scr[s, pl.ds(b, S5_STEPS, stride=batch), :]
                                    for s in range(D_MODEL // LANES)], axis=1).astype(BF16)


def _s5(u_tm, a, b_blk, c_blk, d, batch, seq):
    rows = S5_STEPS * batch
    w = S5_SLAB_STATE
    return pl.pallas_call(
        functools.partial(_s5_kernel, batch=batch),
        grid=(seq // S5_STEPS,),
        in_specs=[
            pl.BlockSpec((rows, D_MODEL), lambda c: (c, 0)),
            pl.BlockSpec((S5_SLABS, 2, w), lambda c: (0, 0, 0)),
            pl.BlockSpec((S5_SLABS, MXU_DIM, 2 * w), lambda c: (0, 0, 0)),
            pl.BlockSpec((S5_SLABS, 2 * w, MXU_DIM), lambda c: (0, 0, 0)),
            pl.BlockSpec((1, D_MODEL), lambda c: (0, 0)),
        ],
        out_specs=pl.BlockSpec((batch, S5_STEPS, D_MODEL), lambda c: (0, c, 0)),
        out_shape=jax.ShapeDtypeStruct((batch, seq, D_MODEL), BF16),
        scratch_shapes=[
            pltpu.VMEM((rows, S5_SLABS * 2 * w), F32),
            pltpu.VMEM((batch, S5_SLABS * 2 * w), F32),
            pltpu.VMEM((D_MODEL // LANES, rows, LANES), F32),
        ],
        compiler_params=_params(("arbitrary",)),
        name="s5_scan",
    )(u_tm, a, b_blk, c_blk, d)


def _conv_kernel(c_ref, halo_ref, w_ref, b_ref, lg_ref, lb_ref, o_ref,
                 xe_scr, xs_scr, wb_scr, y_scr, *, nsb):
    tm = c_ref.shape[0]
    strips = D_MODEL // LANES
    first = (pl.program_id(0) % nsb) == 0
    xe_scr[0:CONV_HALO, :] = jnp.where(first, 0.0, halo_ref[...].astype(F32))
    xe_scr[CONV_HALO:, :] = c_ref[...].astype(F32)
    for b in range(SUBLANES):
        span = tm + CONV_HALO - (SUBLANES if b else 0)
        for cb in range(strips):
            xs_scr[b, cb, 0:span, :] = xe_scr[b:b + span, cb * LANES:(cb + 1) * LANES]
    for cb in range(strips):
        cols = slice(cb * LANES, (cb + 1) * LANES)
        for k in range(CONV_KERNEL):
            wb_scr[k, cb] = jnp.broadcast_to(w_ref[k:k + 1, cols], (SUBLANES, LANES))
        wb_scr[CONV_KERNEL, cb] = jnp.broadcast_to(b_ref[:, cols], (SUBLANES, LANES))
    lead = CONV_HALO - (CONV_KERNEL - 1)
    reps = tm // SUBLANES

    def strip(cb, carry):
        acc = jnp.tile(wb_scr[CONV_KERNEL, cb], (reps, 1))
        for k in range(CONV_KERNEL):
            shift = (lead + k) % SUBLANES
            base = lead + k - shift
            acc = acc + jnp.tile(wb_scr[k, cb], (reps, 1)) * xs_scr[shift, cb, base:base + tm, :]
        y_scr[cb] = acc
        return carry

    lax.fori_loop(0, strips, strip, 0)
    y = jnp.concatenate([y_scr[cb] for cb in range(strips)], axis=1)
    yc = y - jnp.mean(y, axis=-1, keepdims=True)
    yn = yc * lax.rsqrt(jnp.mean(yc * yc, axis=-1, keepdims=True) + EPS) * lg_ref[...] + lb_ref[...]
    o_ref[...] = (yn * jax.nn.sigmoid(yn)).astype(BF16)


def _conv_module(c, w, b, lg, lb, seq):
    n = c.shape[0]
    tm = CONV_TILE
    nsb = seq // tm
    hb = tm // CONV_HALO
    vec = pl.BlockSpec((1, D_MODEL), lambda i: (0, 0))
    return pl.pallas_call(
        functools.partial(_conv_kernel, nsb=nsb),
        grid=(n // tm,),
        in_specs=[
            pl.BlockSpec((tm, D_MODEL), lambda i: (i, 0)),
            pl.BlockSpec((CONV_HALO, D_MODEL), lambda i: (jnp.maximum(i * hb - 1, 0), 0)),
            pl.BlockSpec((CONV_KERNEL, D_MODEL), lambda i: (0, 0)),
            vec, vec, vec,
        ],
        out_specs=pl.BlockSpec((tm, D_MODEL), lambda i: (i, 0)),
        out_shape=jax.ShapeDtypeStruct((n, D_MODEL), BF16),
        scratch_shapes=[pltpu.VMEM((tm + CONV_HALO, D_MODEL), F32),
                        pltpu.VMEM((SUBLANES, D_MODEL // LANES, tm + CONV_HALO, LANES), F32),
                        pltpu.VMEM((CONV_KERNEL + 1, D_MODEL // LANES, SUBLANES, LANES), F32),
                        pltpu.VMEM((D_MODEL // LANES, tm, LANES), F32)],
        compiler_params=_params(("parallel",)),
        name="conv_module",
    )(c, c, w, b, lg, lb)


def _merge_kernel(o_ref, y_ref, c_ref, g_ref, x_ref, wa_ref, w1_ref, w2_ref, wc_ref, wo_ref, out_ref):
    dot = functools.partial(jnp.dot, preferred_element_type=F32)
    y = y_ref[...]
    mix = g_ref[:, 0:D_MODEL].astype(F32) * dot(o_ref[...], wa_ref[...])
    mix = mix + g_ref[:, D_MODEL:2 * D_MODEL].astype(F32) * (
        dot(y, w1_ref[...]) * jax.nn.sigmoid(dot(y, w2_ref[...])))
    mix = mix + g_ref[:, 2 * D_MODEL:].astype(F32) * dot(c_ref[...], wc_ref[...])
    out_ref[...] = x_ref[...] + dot(mix.astype(BF16), wo_ref[...])


def _merge(o, y, c, gates, x2d, wa, w1, w2, wc, wo):
    n = x2d.shape[0]
    tm = ROW_TILE
    row = pl.BlockSpec((tm, D_MODEL), lambda i: (i, 0))
    wgt = pl.BlockSpec((D_MODEL, D_MODEL), lambda i: (0, 0))
    return pl.pallas_call(
        _merge_kernel,
        grid=(n // tm,),
        in_specs=[
            row, row, row,
            pl.BlockSpec((tm, 3 * D_MODEL), lambda i: (i, 0)),
            row,
            wgt, wgt, wgt, wgt, wgt,
        ],
        out_specs=row,
        out_shape=jax.ShapeDtypeStruct((n, D_MODEL), F32),
        compiler_params=_params(("parallel",)),
        name="merge",
    )(o, y, c, gates, x2d, wa, w1, w2, wc, wo)


def _ffn_kernel(x_ref, xh_ref, g_ref, wu_ref, dw_ref, wd_ref, out_ref,
                h_scr, ua_scr, ub_scr, acc_scr, *, nsb):
    tm = x_ref.shape[0]
    first = (pl.program_id(0) % nsb) == 0
    g = g_ref[...]
    h_scr[0:FFN_HALO, :] = jnp.where(first, 0.0, _rms_rows(xh_ref[...], g)).astype(BF16)
    h_scr[FFN_HALO:, :] = _rms_rows(x_ref[...], g).astype(BF16)
    lead = FFN_HALO - (FFN_KERNEL - 1)

    def taps(scr, dw):
        out = dw[0:1, :] * scr[lead:lead + tm, :]
        for k in range(1, FFN_KERNEL):
            out = out + dw[k:k + 1, :] * scr[lead + k:lead + k + tm, :]
        return out

    for j in range(FFN_TILES):
        lin = slice(j * FFN_TILE, (j + 1) * FFN_TILE)
        gate = slice(FFN_HIDDEN + j * FFN_TILE, FFN_HIDDEN + (j + 1) * FFN_TILE)
        h = h_scr[...]
        ua_scr[...] = jnp.dot(h, wu_ref[:, lin], preferred_element_type=F32)
        ub_scr[...] = jnp.dot(h, wu_ref[:, gate], preferred_element_type=F32)
        act = jax.nn.gelu(taps(ub_scr, dw_ref[:, gate])) * taps(ua_scr, dw_ref[:, lin])
        part = jnp.dot(act.astype(BF16), wd_ref[lin, :], preferred_element_type=F32)
        if j == 0:
            acc_scr[...] = part
        else:
            acc_scr[...] += part
    out_ref[...] = x_ref[...] + acc_scr[...]


def _ffn(x2d, g, wu, dw, wd, seq):
    n = x2d.shape[0]
    tm = ROW_TILE
    nsb = seq // tm
    hb = tm // FFN_HALO
    const = lambda i: (0, 0)
    return pl.pallas_call(
        functools.partial(_ffn_kernel, nsb=nsb),
        grid=(n // tm,),
        in_specs=[
            pl.BlockSpec((tm, D_MODEL), lambda i: (i, 0)),
            pl.BlockSpec((FFN_HALO, D_MODEL), lambda i: (jnp.maximum(i * hb - 1, 0), 0)),
            pl.BlockSpec((1, D_MODEL), lambda i: (0, 0)),
            pl.BlockSpec((D_MODEL, 2 * FFN_HIDDEN), const),
            pl.BlockSpec((FFN_KERNEL, 2 * FFN_HIDDEN), const),
            pl.BlockSpec((FFN_HIDDEN, D_MODEL), const),
        ],
        out_specs=pl.BlockSpec((tm, D_MODEL), lambda i: (i, 0)),
        out_shape=jax.ShapeDtypeStruct((n, D_MODEL), F32),
        scratch_shapes=[
            pltpu.VMEM((tm + FFN_HALO, D_MODEL), BF16),
            pltpu.VMEM((tm + FFN_HALO, FFN_TILE), F32),
            pltpu.VMEM((tm + FFN_HALO, FFN_TILE), F32),
            pltpu.VMEM((tm, D_MODEL), F32),
        ],
        compiler_params=_params(("parallel",)),
        name="ffn",
    )(x2d, x2d, g, wu, dw, wd)


def kernel(x, norm_mix, w_in, qk_gain_q, qk_gain_k, lambda_q1, lambda_k1, lambda_q2, lambda_k2, diff_subln, rel_bias, w_attn_out, s5_lambda_re, s5_lambda_im, s5_log_step, s5_b_re, s5_b_im, s5_c_re, s5_c_im, s5_d, s5_glu_w1, s5_glu_w2, conv_dw_w, conv_dw_b, conv_ln_g, conv_ln_b, conv_w_out, w_out, norm_ffn, ffn_w_up, ffn_dw_w, ffn_w_down):
    batch, seq, _ = x.shape
    assert batch == SUBLANES, "the S5 scan keeps one batch entry per sublane"
    n = batch * seq
    row = lambda p: p.reshape(1, -1).astype(F32)
    bias = _bias_tiles(rel_bias.astype(F32))
    x2d = x.reshape(n, D_MODEL)
    for l in range(DEPTH):
        lam_init = 0.8 - 0.6 * math.exp(-0.3 * l)
        q, k, v, u_tm, c_glu, gates = _in_proj(
            x2d.reshape(batch, seq, D_MODEL), row(norm_mix[l]), w_in[l].astype(BF16),
            row(jnp.tile(qk_gain_q[l], 2)), row(jnp.tile(qk_gain_k[l], 2)))

        lamv = jnp.stack([lambda_q1[l], lambda_k1[l], lambda_q2[l], lambda_k2[l]]).astype(F32)
        o = _attention(q, k, v, bias, lamv, row(diff_subln[l]), lam_init)

        a, b_blk, c_blk = _s5_discretise(s5_lambda_re[l], s5_lambda_im[l], s5_log_step[l],
                                         s5_b_re[l], s5_b_im[l], s5_c_re[l], s5_c_im[l])
        y = _s5(u_tm.reshape(seq * batch, D_MODEL), a, b_blk, c_blk, row(s5_d[l]), batch, seq)

        c = _conv_module(c_glu.reshape(n, D_MODEL), conv_dw_w[l].astype(F32), row(conv_dw_b[l]),
                         row(conv_ln_g[l]), row(conv_ln_b[l]), seq)

        x2d = _merge(o.reshape(n, D_MODEL), y.reshape(n, D_MODEL), c,
                     gates.reshape(n, 3 * D_MODEL), x2d,
                     w_attn_out[l].astype(BF16), s5_glu_w1[l].astype(BF16),
                     s5_glu_w2[l].astype(BF16), conv_w_out[l].astype(BF16), w_out[l].astype(BF16))

        x2d = _ffn(x2d, row(norm_ffn[l]), ffn_w_up[l].astype(BF16), ffn_dw_w[l].astype(F32),
                   ffn_w_down[l].astype(BF16), seq)
    return x2d.reshape(batch, seq, D_MODEL)
```

```python
import functools
import math

import jax
import jax.numpy as jnp
from jax import lax
from jax.experimental import pallas as pl
from jax.experimental.pallas import tpu as pltpu

F32 = jnp.float32
BF16 = jnp.bfloat16

D_MODEL = 1024
DEPTH = 2
CHUNK = 64
HEADS = 8
HEAD_DIM = 64
HEAD_WIDTH = 2 * HEAD_DIM
S5_GROUP = 16
S5_GROUPS = 64
S5_STATE = 64
CONV_KERNEL = 31
FFN_HIDDEN = 2816
FFN_KERNEL = 3
REL_BUCKETS = 32
REL_MAX_DIST = 128
EPS = 1e-6
IN_TILES = 9
NEG_INF = -1e30

LANES = 128
SUBLANES = 8
MXU_DIM = 256
VMEM_LIMIT = 56 * 1024 * 1024

ROW_TILE = 512
IN_SEQ_TILE = 128
ATTN_Q = 512
ATTN_K = 256
ATTN_KQ = ATTN_Q // ATTN_K
ATTN_GROUPS = 2
LOG2E = math.log2(math.e)
S5_STEPS = 64
S5_SLABS = D_MODEL // MXU_DIM
S5_SLAB_STATE = (MXU_DIM // S5_GROUP) * S5_STATE
CONV_HALO = 32
CONV_TILE = 256
FFN_HALO = 16
FFN_TILE = 256
FFN_TILES = FFN_HIDDEN // FFN_TILE


def _params(semantics):
    return pltpu.CompilerParams(dimension_semantics=semantics, vmem_limit_bytes=VMEM_LIMIT)


def _sigmoid(x):
    return 0.5 * jnp.tanh(0.5 * x) + 0.5


def _rms_rows(x, g):
    return x * lax.rsqrt(jnp.mean(x * x, axis=-1, keepdims=True) + EPS) * g


def _head_norm(z, gain, scale):
    lane = lax.broadcasted_iota(jnp.int32, (1, HEAD_WIDTH), 1)
    first = lane < HEAD_DIM
    outs = []
    for h in range(z.shape[1] // HEAD_WIDTH):
        slab = z[:, h * HEAD_WIDTH:(h + 1) * HEAD_WIDTH]
        sq = slab * slab
        s0 = jnp.sum(jnp.where(first, sq, 0.0), axis=-1, keepdims=True)
        s1 = jnp.sum(jnp.where(first, 0.0, sq), axis=-1, keepdims=True)
        r = lax.rsqrt(jnp.where(first, s0, s1) * (1.0 / HEAD_DIM) + EPS)
        outs.append((slab * r * gain * scale).astype(BF16))
    return jnp.concatenate(outs, axis=-1)


def _in_proj_kernel(x_ref, g_ref, w_ref, gq_ref, gk_ref,
                    q_ref, k_ref, v_ref, u_ref, c_ref, gate_ref, h_scr, a_scr):
    j = pl.program_id(1)
    batch, ts, _ = x_ref.shape
    rows = batch * ts

    @pl.when(j == 0)
    def _():
        x = x_ref[...].reshape(rows, D_MODEL)
        h_scr[...] = _rms_rows(x, g_ref[...]).astype(BF16)

    def project(epilogue):
        for c in range(D_MODEL // MXU_DIM):
            cols = slice(c * MXU_DIM, (c + 1) * MXU_DIM)
            epilogue(cols, jnp.dot(h_scr[...], w_ref[:, cols], preferred_element_type=F32))

    def store(ref, fn):
        def epilogue(cols, z):
            val = fn(cols, z)
            ref[:, :, cols] = val.reshape(batch, ts, MXU_DIM)
        return epilogue

    @pl.when(j == 0)
    def _():
        project(store(q_ref, lambda cols, z: _head_norm(z, gq_ref[...], HEAD_DIM ** -0.5 * LOG2E)))

    @pl.when(j == 1)
    def _():
        project(store(k_ref, lambda cols, z: _head_norm(z, gk_ref[...], 1.0)))

    @pl.when(j == 2)
    def _():
        project(store(v_ref, lambda cols, z: z.astype(BF16)))

    def to_strips(cols, z):
        for s in range(MXU_DIM // LANES):
            a_scr[cols.start // LANES + s] = z[:, s * LANES:(s + 1) * LANES]

    def from_strips(cols):
        first = cols.start // LANES
        return jnp.concatenate([a_scr[first + s] for s in range(MXU_DIM // LANES)], axis=1)

    @pl.when(j == 3)
    def _():
        project(to_strips)

        def to_time_major(t, carry):
            u_ref[t] = jnp.concatenate([a_scr[s, pl.ds(t, batch, stride=ts), :]
                                        for s in range(D_MODEL // LANES)], axis=1)
            return carry

        lax.fori_loop(0, ts, to_time_major, 0, unroll=16)

    @pl.when(j == 4)
    def _():
        project(to_strips)

    @pl.when(j == 5)
    def _():
        project(store(c_ref, lambda cols, z: (from_strips(cols) * _sigmoid(z)).astype(BF16)))

    @pl.when(j >= 6)
    def _():
        project(store(gate_ref, lambda cols, z: _sigmoid(z).astype(BF16)))


def _in_proj(x, g, w, gq, gk):
    batch, seq, _ = x.shape
    ts = IN_SEQ_TILE
    rows = batch * ts
    blk = (batch, ts, D_MODEL)
    tile = lambda i, j: (0, i, 0)
    act = jax.ShapeDtypeStruct((batch, seq, D_MODEL), BF16)
    return pl.pallas_call(
        _in_proj_kernel,
        grid=(seq // ts, IN_TILES),
        in_specs=[
            pl.BlockSpec(blk, tile),
            pl.BlockSpec((1, D_MODEL), lambda i, j: (0, 0)),
            pl.BlockSpec((D_MODEL, D_MODEL), lambda i, j: (0, j)),
            pl.BlockSpec((1, HEAD_WIDTH), lambda i, j: (0, 0)),
            pl.BlockSpec((1, HEAD_WIDTH), lambda i, j: (0, 0)),
        ],
        out_specs=[
            pl.BlockSpec(blk, tile),
            pl.BlockSpec(blk, tile),
            pl.BlockSpec(blk, tile),
            pl.BlockSpec((ts, batch, D_MODEL), lambda i, j: (i, 0, 0)),
            pl.BlockSpec(blk, tile),
            pl.BlockSpec(blk, lambda i, j: (0, i, jnp.maximum(j - 6, 0))),
        ],
        out_shape=[
            act, act, act,
            jax.ShapeDtypeStruct((seq, batch, D_MODEL), F32),
            act,
            jax.ShapeDtypeStruct((batch, seq, 3 * D_MODEL), BF16),
        ],
        scratch_shapes=[pltpu.VMEM((rows, D_MODEL), BF16),
                        pltpu.VMEM((D_MODEL // LANES, rows, LANES), F32)],
        compiler_params=_params(("parallel", "arbitrary")),
        name="in_proj",
    )(x, g, w, gq, gk)


def _t5_bucket_ids(rel):
    nb = REL_BUCKETS // 2
    n = -rel
    ret = jnp.where(n < 0, nb, 0)
    n = jnp.abs(n)
    max_exact = nb // 2
    nf = jnp.maximum(n, 1).astype(F32)
    large = max_exact + (jnp.log(nf / max_exact) / math.log(REL_MAX_DIST / max_exact)
                         * (nb - max_exact)).astype(jnp.int32)
    large = jnp.minimum(large, nb - 1)
    return ret + jnp.where(n < max_exact, n, large)


def _bias_kernel(far_ref, tab_ref, ids_ref, out_ref):
    h = pl.program_id(0)
    tq, tk = ATTN_Q, ATTN_K
    ids = ids_ref[...]
    acc = jnp.zeros(ids.shape, F32)
    for b in range(REL_BUCKETS):
        acc = jnp.where(ids == b, tab_ref[b, h], acc)
    acc = (acc - tab_ref[far_ref[0], h]) * LOG2E
    qi = lax.broadcasted_iota(jnp.int32, (tq, tk), 0)
    kj = lax.broadcasted_iota(jnp.int32, (tq, tk), 1)
    out_ref[0] = jnp.zeros((2 * tq, tk), F32)
    for n, o in enumerate(range(-1, ATTN_KQ)):
        visible = ((o + 1) * tk + kj) // CHUNK <= (tk + qi) // CHUNK
        tile = jnp.where(visible, acc[n], NEG_INF)
        out_ref[n + 1, 0:tq] = tile
        out_ref[n + 1, tq:2 * tq] = tile


def _bias_tiles(rel_bias):
    tq, tk = ATTN_Q, ATTN_K
    qi = jnp.arange(tq, dtype=jnp.int32)[:, None]
    kj = jnp.arange(tk, dtype=jnp.int32)[None, :]
    ids = jnp.stack([_t5_bucket_ids(o * tk + kj - qi) for o in range(-1, ATTN_KQ)])
    far = _t5_bucket_ids(jnp.full((1,), -(tk + 1), jnp.int32))
    n_tiles = ATTN_KQ + 2
    return pl.pallas_call(
        _bias_kernel,
        grid=(HEADS,),
        in_specs=[
            pl.BlockSpec(memory_space=pltpu.SMEM),
            pl.BlockSpec(memory_space=pltpu.SMEM),
            pl.BlockSpec((ATTN_KQ + 1, tq, tk), lambda h: (0, 0, 0)),
        ],
        out_specs=pl.BlockSpec((None, n_tiles, 2 * tq, tk), lambda h: (h, 0, 0, 0)),
        out_shape=jax.ShapeDtypeStruct((HEADS, n_tiles, 2 * tq, tk), F32),
        compiler_params=_params(("parallel",)),
        name="bias_tiles",
    )(far, rel_bias, ids)


def _attn_kernel(lamv_ref, subln_ref, bias_ref, q_ref, k_ref, v_ref, o_ref,
                 s_scr, vx_scr, m_scr, acc_scr, *, lam_init):
    tq, tk = ATTN_Q, ATTN_K
    i = pl.program_id(2)
    n_groups = i + 1

    @pl.when(i == 0)
    def _():
        vx_scr[:, 0:HEAD_WIDTH] = v_ref[...]
        vx_scr[:, HEAD_WIDTH:] = jnp.ones((vx_scr.shape[0], HEAD_WIDTH), BF16)

    q = q_ref[...]
    lane = lax.broadcasted_iota(jnp.int32, (1, HEAD_WIDTH), 1)
    first = lane < HEAD_DIM
    zero = jnp.zeros_like(q)
    qs = jnp.concatenate([jnp.where(first, q, zero), jnp.where(first, zero, q)], axis=0)

    def rows(j):
        return pl.ds(pl.multiple_of(j * tk, tk), tk)

    def score_tile(j):
        s = lax.dot_general(qs, k_ref[rows(j), :], (((1,), (1,)), ((), ())),
                            preferred_element_type=F32)
        s = s + bias_ref[jnp.clip(j - ATTN_KQ * i + 2, 0, ATTN_KQ + 1)]
        s_scr[j] = s
        return jnp.maximum(s[:, 0:LANES], s[:, LANES:2 * LANES])

    m_scr[...] = jnp.full(m_scr.shape, NEG_INF, F32)

    def sweep(groups_fn):
        def body(p, carry):
            groups_fn(p * ATTN_GROUPS, ATTN_GROUPS)
            return carry

        def tail(g, carry):
            groups_fn(g, 1)
            return carry

        n_full = n_groups // ATTN_GROUPS
        lax.fori_loop(0, n_full, body, 0)
        lax.fori_loop(n_full * ATTN_GROUPS, n_groups, tail, 0)

    def max_groups(g0, count):
        tile_max = [score_tile(g0 * ATTN_KQ + u) for u in range(count * ATTN_KQ)]
        m_scr[...] = functools.reduce(jnp.maximum, tile_max, m_scr[...])

    sweep(max_groups)
    m_scr[...] = jnp.broadcast_to(jnp.max(m_scr[...], axis=-1, keepdims=True), m_scr.shape)
    acc_scr[...] = jnp.zeros(acc_scr.shape, F32)

    def prob_tile(j):
        m_rep = m_scr[...]
        p = jnp.exp2(s_scr[j] - jnp.concatenate([m_rep, m_rep], axis=1))
        return jnp.dot(p.astype(BF16), vx_scr[rows(j), :], preferred_element_type=F32)

    def sum_groups(g0, count):
        parts = [prob_tile(g0 * ATTN_KQ + u) for u in range(count * ATTN_KQ)]
        acc_scr[...] += functools.reduce(lambda a, b: a + b, parts)

    sweep(sum_groups)

    lv = lamv_ref[...]
    lam = (jnp.exp(jnp.sum(lv[0:1] * lv[1:2], axis=-1, keepdims=True))
           - jnp.exp(jnp.sum(lv[2:3] * lv[3:4], axis=-1, keepdims=True)) + lam_init)
    pv, l = acc_scr[:, 0:HEAD_WIDTH], acc_scr[:, HEAD_WIDTH:]
    o = pv[0:tq] / l[0:tq] - lam * (pv[tq:2 * tq] / l[tq:2 * tq])
    o_ref[...] = (_rms_rows(o, subln_ref[...]) * (1.0 - lam_init)).astype(BF16)


def _attention(q, k, v, bias, lamv, subln, lam_init):
    b, s, _ = q.shape
    tq, tk = ATTN_Q, ATTN_K
    return pl.pallas_call(
        functools.partial(_attn_kernel, lam_init=lam_init),
        grid=(b, HEADS, s // tq),
        in_specs=[
            pl.BlockSpec((4, HEAD_DIM), lambda bi, h, i: (0, 0)),
            pl.BlockSpec((1, HEAD_WIDTH), lambda bi, h, i: (0, 0)),
            pl.BlockSpec((None, ATTN_KQ + 2, 2 * tq, tk), lambda bi, h, i: (h, 0, 0, 0)),
            pl.BlockSpec((None, tq, HEAD_WIDTH), lambda bi, h, i: (bi, i, h)),
            pl.BlockSpec((None, s, HEAD_WIDTH), lambda bi, h, i: (bi, 0, h)),
            pl.BlockSpec((None, s, HEAD_WIDTH), lambda bi, h, i: (bi, 0, h)),
        ],
        out_specs=pl.BlockSpec((None, tq, HEAD_WIDTH), lambda bi, h, i: (bi, i, h)),
        out_shape=jax.ShapeDtypeStruct((b, s, D_MODEL), BF16),
        scratch_shapes=[
            pltpu.VMEM((s // tk, 2 * tq, tk), F32),
            pltpu.VMEM((s, 2 * HEAD_WIDTH), BF16),
            pltpu.VMEM((2 * tq, LANES), F32),
            pltpu.VMEM((2 * tq, 2 * HEAD_WIDTH), F32),
        ],
        compiler_params=_params(("parallel", "parallel", "arbitrary")),
        name="diff_attention",
    )(lamv, subln, bias, q, k, v)


def _s5_discretise(lam_re, lam_im, log_step, b_re, b_im, c_re, c_im):
    step = jnp.exp(log_step.astype(F32))[:, None]
    lr, li = lam_re.astype(F32), lam_im.astype(F32)
    mag = jnp.exp(lr * step)
    ab_re, ab_im = mag * jnp.cos(li * step), mag * jnp.sin(li * step)
    den = lr * lr + li * li
    nr, ni = ab_re - 1.0, ab_im
    f_re = (nr * lr + ni * li) / den
    f_im = (ni * lr - nr * li) / den
    br, bi = b_re.astype(F32), b_im.astype(F32)
    bb_re = f_re[..., None] * br - f_im[..., None] * bi
    bb_im = f_re[..., None] * bi + f_im[..., None] * br
    gs = MXU_DIM // S5_GROUP
    state = jnp.arange(S5_SLAB_STATE, dtype=jnp.int32)
    chan = jnp.arange(MXU_DIM, dtype=jnp.int32)
    spread = (state[None, :] % S5_STATE == jnp.arange(S5_STATE, dtype=jnp.int32)[:, None]).astype(BF16)
    own = (chan[:, None] // S5_GROUP) == (state[None, :] // S5_STATE)

    def in_blocks(bb):
        compact = bb.reshape(S5_SLABS, gs, S5_STATE, S5_GROUP).transpose(0, 1, 3, 2)
        compact = compact.reshape(S5_SLABS, MXU_DIM, S5_STATE).astype(BF16)
        blk = jnp.einsum('krp,pc->krc', compact, spread, preferred_element_type=F32)
        return jnp.where(own, blk, 0.0).astype(BF16)

    def out_blocks(c):
        compact = c.reshape(S5_SLABS, MXU_DIM, S5_STATE).astype(BF16)
        blk = jnp.einsum('pc,krp->kcr', spread, compact, preferred_element_type=F32)
        return jnp.where(own.T, blk, 0.0).astype(BF16)

    b_blk = jnp.concatenate([in_blocks(bb_re), in_blocks(bb_im)], axis=2)
    c_blk = jnp.concatenate([out_blocks(c_re.astype(F32)), out_blocks(-c_im.astype(F32))], axis=1)
    a = jnp.stack([ab_re.reshape(S5_SLABS, S5_SLAB_STATE), ab_im.reshape(S5_SLABS, S5_SLAB_STATE)],
                  axis=1)
    return a, b_blk, c_blk


def _s5_kernel(u_ref, a_ref, b_ref, c_ref, d_ref, y_ref, x_scr, st_scr, y_scr, *, batch):
    w = S5_SLAB_STATE

    @pl.when(pl.program_id(0) == 0)
    def _():
        st_scr[...] = jnp.zeros(st_scr.shape, F32)

    u = u_ref[...]
    ub = u.astype(BF16)
    for kt in range(S5_SLABS):
        x_scr[:, kt * 2 * w:(kt + 1) * 2 * w] = jnp.dot(
            ub[:, kt * MXU_DIM:(kt + 1) * MXU_DIM], b_ref[kt], preferred_element_type=F32)

    for kt in range(S5_SLABS):
        re = slice(kt * 2 * w, kt * 2 * w + w)
        im = slice(kt * 2 * w + w, (kt + 1) * 2 * w)
        ar = jnp.broadcast_to(a_ref[kt, 0:1, :], (batch, w))
        ai = jnp.broadcast_to(a_ref[kt, 1:2, :], (batch, w))

        def step(t, carry, re=re, im=im, ar=ar, ai=ai):
            xr, xi = carry
            rows = pl.ds(pl.multiple_of(t * batch, batch), batch)
            nxr = ar * xr - ai * xi + x_scr[rows, re]
            nxi = ar * xi + ai * xr + x_scr[rows, im]
            x_scr[rows, re] = nxr
            x_scr[rows, im] = nxi
            return nxr, nxi

        xr, xi = lax.fori_loop(0, S5_STEPS, step, (st_scr[:, re], st_scr[:, im]), unroll=4)
        st_scr[:, re] = xr
        st_scr[:, im] = xi

    for kt in range(S5_SLABS):
        cols = slice(kt * MXU_DIM, (kt + 1) * MXU_DIM)
        y = jnp.dot(x_scr[:, kt * 2 * w:(kt + 1) * 2 * w].astype(BF16), c_ref[kt],
                    preferred_element_type=F32)
        y = y + d_ref[:, cols] * u[:, cols]
        y = jax.nn.gelu(y)
        for s in range(MXU_DIM // LANES):
            y_scr[kt * (MXU_DIM // LANES) + s] = y[:, s * LANES:(s + 1) * LANES]
    for b in range(batch):
        y_ref[b] = jnp.concatenate([y_scr[s, pl.ds(b, S5_STEPS, stride=batch), :]
                                    for s in range(D_MODEL // LANES)], axis=1).astype(BF16)


def _s5(u_tm, a, b_blk, c_blk, d, batch, seq):
    rows = S5_STEPS * batch
    w = S5_SLAB_STATE
    return pl.pallas_call(
        functools.partial(_s5_kernel, batch=batch),
        grid=(seq // S5_STEPS,),
        in_specs=[
            pl.BlockSpec((rows, D_MODEL), lambda c: (c, 0)),
            pl.BlockSpec((S5_SLABS, 2, w), lambda c: (0, 0, 0)),
            pl.BlockSpec((S5_SLABS, MXU_DIM, 2 * w), lambda c: (0, 0, 0)),
            pl.BlockSpec((S5_SLABS, 2 * w, MXU_DIM), lambda c: (0, 0, 0)),
            pl.BlockSpec((1, D_MODEL), lambda c: (0, 0)),
        ],
        out_specs=pl.BlockSpec((batch, S5_STEPS, D_MODEL), lambda c: (0, c, 0)),
        out_shape=jax.ShapeDtypeStruct((batch, seq, D_MODEL), BF16),
        scratch_shapes=[
            pltpu.VMEM((rows, S5_SLABS * 2 * w), F32),
            pltpu.VMEM((batch, S5_SLABS * 2 * w), F32),
            pltpu.VMEM((D_MODEL // LANES, rows, LANES), F32),
        ],
        compiler_params=_params(("arbitrary",)),
        name="s5_scan",
    )(u_tm, a, b_blk, c_blk, d)


def _conv_kernel(c_ref, halo_ref, w_ref, b_ref, lg_ref, lb_ref, o_ref,
                 xe_scr, xs_scr, wb_scr, y_scr, *, nsb):
    tm = c_ref.shape[0]
    strips = D_MODEL // LANES
    first = (pl.program_id(0) % nsb) == 0
    xe_scr[0:CONV_HALO, :] = jnp.where(first, 0.0, halo_ref[...].astype(F32))
    xe_scr[CONV_HALO:, :] = c_ref[...].astype(F32)
    for b in range(SUBLANES):
        span = tm + CONV_HALO - (SUBLANES if b else 0)
        for cb in range(strips):
            xs_scr[b, cb, 0:span, :] = xe_scr[b:b + span, cb * LANES:(cb + 1) * LANES]
    for cb in range(strips):
        cols = slice(cb * LANES, (cb + 1) * LANES)
        for k in range(CONV_KERNEL):
            wb_scr[k, cb] = jnp.broadcast_to(w_ref[k:k + 1, cols], (SUBLANES, LANES))
        wb_scr[CONV_KERNEL, cb] = jnp.broadcast_to(b_ref[:, cols], (SUBLANES, LANES))
    lead = CONV_HALO - (CONV_KERNEL - 1)
    reps = tm // SUBLANES

    def strip(cb, carry):
        acc = jnp.tile(wb_scr[CONV_KERNEL, cb], (reps, 1))
        for k in range(CONV_KERNEL):
            shift = (lead + k) % SUBLANES
            base = lead + k - shift
            acc = acc + jnp.tile(wb_scr[k, cb], (reps, 1)) * xs_scr[shift, cb, base:base + tm, :]
        y_scr[cb] = acc
        return carry

    lax.fori_loop(0, strips, strip, 0)
    y = jnp.concatenate([y_scr[cb] for cb in range(strips)], axis=1)
    yc = y - jnp.mean(y, axis=-1, keepdims=True)
    yn = yc * lax.rsqrt(jnp.mean(yc * yc, axis=-1, keepdims=True) + EPS) * lg_ref[...] + lb_ref[...]
    o_ref[...] = (yn * jax.nn.sigmoid(yn)).astype(BF16)


def _conv_module(c, w, b, lg, lb, seq):
    n = c.shape[0]
    tm = CONV_TILE
    nsb = seq // tm
    hb = tm // CONV_HALO
    vec = pl.BlockSpec((1, D_MODEL), lambda i: (0, 0))
    return pl.pallas_call(
        functools.partial(_conv_kernel, nsb=nsb),
        grid=(n // tm,),
        in_specs=[
            pl.BlockSpec((tm, D_MODEL), lambda i: (i, 0)),
            pl.BlockSpec((CONV_HALO, D_MODEL), lambda i: (jnp.maximum(i * hb - 1, 0), 0)),
            pl.BlockSpec((CONV_KERNEL, D_MODEL), lambda i: (0, 0)),
            vec, vec, vec,
        ],
        out_specs=pl.BlockSpec((tm, D_MODEL), lambda i: (i, 0)),
        out_shape=jax.ShapeDtypeStruct((n, D_MODEL), BF16),
        scratch_shapes=[pltpu.VMEM((tm + CONV_HALO, D_MODEL), F32),
                        pltpu.VMEM((SUBLANES, D_MODEL // LANES, tm + CONV_HALO, LANES), F32),
                        pltpu.VMEM((CONV_KERNEL + 1, D_MODEL // LANES, SUBLANES, LANES), F32),
                        pltpu.VMEM((D_MODEL // LANES, tm, LANES), F32)],
        compiler_params=_params(("parallel",)),
        name="conv_module",
    )(c, c, w, b, lg, lb)


def _merge_kernel(o_ref, y_ref, c_ref, g_ref, x_ref, wa_ref, w1_ref, w2_ref, wc_ref, wo_ref, out_ref):
    dot = functools.partial(jnp.dot, preferred_element_type=F32)
    y = y_ref[...]
    mix = g_ref[:, 0:D_MODEL].astype(F32) * dot(o_ref[...], wa_ref[...])
    mix = mix + g_ref[:, D_MODEL:2 * D_MODEL].astype(F32) * (
        dot(y, w1_ref[...]) * jax.nn.sigmoid(dot(y, w2_ref[...])))
    mix = mix + g_ref[:, 2 * D_MODEL:].astype(F32) * dot(c_ref[...], wc_ref[...])
    out_ref[...] = x_ref[...] + dot(mix.astype(BF16), wo_ref[...])


def _merge(o, y, c, gates, x2d, wa, w1, w2, wc, wo):
    n = x2d.shape[0]
    tm = ROW_TILE
    row = pl.BlockSpec((tm, D_MODEL), lambda i: (i, 0))
    wgt = pl.BlockSpec((D_MODEL, D_MODEL), lambda i: (0, 0))
    return pl.pallas_call(
        _merge_kernel,
        grid=(n // tm,),
        in_specs=[
            row, row, row,
            pl.BlockSpec((tm, 3 * D_MODEL), lambda i: (i, 0)),
            row,
            wgt, wgt, wgt, wgt, wgt,
        ],
        out_specs=row,
        out_shape=jax.ShapeDtypeStruct((n, D_MODEL), F32),
        compiler_params=_params(("parallel",)),
        name="merge",
    )(o, y, c, gates, x2d, wa, w1, w2, wc, wo)


def _ffn_kernel(x_ref, xh_ref, g_ref, wu_ref, dw_ref, wd_ref, out_ref,
                h_scr, acc_scr, *up_scrs, nsb):
    tm = x_ref.shape[0]
    first = (pl.program_id(0) % nsb) == 0
    g = g_ref[...]
    h_scr[0:FFN_HALO, :] = jnp.where(first, 0.0, _rms_rows(xh_ref[...], g)).astype(BF16)
    h_scr[FFN_HALO:, :] = _rms_rows(x_ref[...], g).astype(BF16)
    lead = FFN_HALO - (FFN_KERNEL - 1)

    def taps(scr, dw):
        out = dw[0:1, :] * scr[lead:lead + tm, :]
        for k in range(1, FFN_KERNEL):
            out = out + dw[k:k + 1, :] * scr[lead + k:lead + k + tm, :]
        return out

    def halves(j):
        return (slice(j * FFN_TILE, (j + 1) * FFN_TILE),
                slice(FFN_HIDDEN + j * FFN_TILE, FFN_HIDDEN + (j + 1) * FFN_TILE))

    def up_project(j):
        lin, gate = halves(j)
        h = h_scr[...]
        up_scrs[2 * (j % 2)][...] = jnp.dot(h, wu_ref[:, lin], preferred_element_type=F32)
        up_scrs[2 * (j % 2) + 1][...] = jnp.dot(h, wu_ref[:, gate], preferred_element_type=F32)

    up_project(0)
    for j in range(FFN_TILES):
        lin, gate = halves(j)
        if j + 1 < FFN_TILES:
            up_project(j + 1)
        ua_scr, ub_scr = up_scrs[2 * (j % 2)], up_scrs[2 * (j % 2) + 1]
        act = jax.nn.gelu(taps(ub_scr, dw_ref[:, gate])) * taps(ua_scr, dw_ref[:, lin])
        part = jnp.dot(act.astype(BF16), wd_ref[lin, :], preferred_element_type=F32)
        if j == 0:
            acc_scr[...] = part
        else:
            acc_scr[...] += part
    out_ref[...] = x_ref[...] + acc_scr[...]


def _ffn(x2d, g, wu, dw, wd, seq):
    n = x2d.shape[0]
    tm = ROW_TILE
    nsb = seq // tm
    hb = tm // FFN_HALO
    const = lambda i: (0, 0)
    return pl.pallas_call(
        functools.partial(_ffn_kernel, nsb=nsb),
        grid=(n // tm,),
        in_specs=[
            pl.BlockSpec((tm, D_MODEL), lambda i: (i, 0)),
            pl.BlockSpec((FFN_HALO, D_MODEL), lambda i: (jnp.maximum(i * hb - 1, 0), 0)),
            pl.BlockSpec((1, D_MODEL), lambda i: (0, 0)),
            pl.BlockSpec((D_MODEL, 2 * FFN_HIDDEN), const),
            pl.BlockSpec((FFN_KERNEL, 2 * FFN_HIDDEN), const),
            pl.BlockSpec((FFN_HIDDEN, D_MODEL), const),
        ],
        out_specs=pl.BlockSpec((tm, D_MODEL), lambda i: (i, 0)),
        out_shape=jax.ShapeDtypeStruct((n, D_MODEL), F32),
        scratch_shapes=[
            pltpu.VMEM((tm + FFN_HALO, D_MODEL), BF16),
            pltpu.VMEM((tm, D_MODEL), F32),
        ] + [pltpu.VMEM((tm + FFN_HALO, FFN_TILE), F32)] * 4 + [
        ],
        compiler_params=_params(("parallel",)),
        name="ffn",
    )(x2d, x2d, g, wu, dw, wd)


def kernel(x, norm_mix, w_in, qk_gain_q, qk_gain_k, lambda_q1, lambda_k1, lambda_q2, lambda_k2, diff_subln, rel_bias, w_attn_out, s5_lambda_re, s5_lambda_im, s5_log_step, s5_b_re, s5_b_im, s5_c_re, s5_c_im, s5_d, s5_glu_w1, s5_glu_w2, conv_dw_w, conv_dw_b, conv_ln_g, conv_ln_b, conv_w_out, w_out, norm_ffn, ffn_w_up, ffn_dw_w, ffn_w_down):
    batch, seq, _ = x.shape
    assert batch == SUBLANES, "the S5 scan keeps one batch entry per sublane"
    n = batch * seq
    row = lambda p: p.reshape(1, -1).astype(F32)
    bias = _bias_tiles(rel_bias.astype(F32))
    x2d = x.reshape(n, D_MODEL)
    for l in range(DEPTH):
        lam_init = 0.8 - 0.6 * math.exp(-0.3 * l)
        q, k, v, u_tm, c_glu, gates = _in_proj(
            x2d.reshape(batch, seq, D_MODEL), row(norm_mix[l]), w_in[l].astype(BF16),
            row(jnp.tile(qk_gain_q[l], 2)), row(jnp.tile(qk_gain_k[l], 2)))

        lamv = jnp.stack([lambda_q1[l], lambda_k1[l], lambda_q2[l], lambda_k2[l]]).astype(F32)
        o = _attention(q, k, v, bias, lamv, row(diff_subln[l]), lam_init)

        a, b_blk, c_blk = _s5_discretise(s5_lambda_re[l], s5_lambda_im[l], s5_log_step[l],
                                         s5_b_re[l], s5_b_im[l], s5_c_re[l], s5_c_im[l])
        y = _s5(u_tm.reshape(seq * batch, D_MODEL), a, b_blk, c_blk, row(s5_d[l]), batch, seq)

        c = _conv_module(c_glu.reshape(n, D_MODEL), conv_dw_w[l].astype(F32), row(conv_dw_b[l]),
                         row(conv_ln_g[l]), row(conv_ln_b[l]), seq)

        x2d = _merge(o.reshape(n, D_MODEL), y.reshape(n, D_MODEL), c,
                     gates.reshape(n, 3 * D_MODEL), x2d,
                     w_attn_out[l].astype(BF16), s5_glu_w1[l].astype(BF16),
                     s5_glu_w2[l].astype(BF16), conv_w_out[l].astype(BF16), w_out[l].astype(BF16))

        x2d = _ffn(x2d, row(norm_ffn[l]), ffn_w_up[l].astype(BF16), ffn_dw_w[l].astype(F32),
                   ffn_w_down[l].astype(BF16), seq)
    return x2d.reshape(batch, seq, D_MODEL)
```

```python
import functools
import math

import jax
import jax.numpy as jnp
from jax import lax
from jax.experimental import pallas as pl
from jax.experimental.pallas import tpu as pltpu

F32 = jnp.float32
BF16 = jnp.bfloat16

D_MODEL = 1024
DEPTH = 2
CHUNK = 64
HEADS = 8
HEAD_DIM = 64
HEAD_WIDTH = 2 * HEAD_DIM
S5_GROUP = 16
S5_GROUPS = 64
S5_STATE = 64
CONV_KERNEL = 31
FFN_HIDDEN = 2816
FFN_KERNEL = 3
REL_BUCKETS = 32
REL_MAX_DIST = 128
EPS = 1e-6
IN_TILES = 9
NEG_INF = -1e30

LANES = 128
SUBLANES = 8
MXU_DIM = 256
VMEM_LIMIT = 56 * 1024 * 1024

ROW_TILE = 512
IN_SEQ_TILE = 128
ATTN_Q = 512
ATTN_K = 256
ATTN_KQ = ATTN_Q // ATTN_K
ATTN_GROUPS = (4, 2, 1)
LOG2E = math.log2(math.e)
S5_STEPS = 64
S5_SLABS = D_MODEL // MXU_DIM
S5_SLAB_STATE = (MXU_DIM // S5_GROUP) * S5_STATE
CONV_HALO = 32
CONV_TILE = 256
FFN_HALO = 16
FFN_TILE = 256
FFN_TILES = FFN_HIDDEN // FFN_TILE


def _params(semantics):
    return pltpu.CompilerParams(dimension_semantics=semantics, vmem_limit_bytes=VMEM_LIMIT)


def _sigmoid(x):
    return 0.5 * jnp.tanh(0.5 * x) + 0.5


def _rms_rows(x, g):
    return x * lax.rsqrt(jnp.mean(x * x, axis=-1, keepdims=True) + EPS) * g


def _head_norm(z, gain, scale):
    lane = lax.broadcasted_iota(jnp.int32, (1, HEAD_WIDTH), 1)
    first = lane < HEAD_DIM
    outs = []
    for h in range(z.shape[1] // HEAD_WIDTH):
        slab = z[:, h * HEAD_WIDTH:(h + 1) * HEAD_WIDTH]
        sq = slab * slab
        s0 = jnp.sum(jnp.where(first, sq, 0.0), axis=-1, keepdims=True)
        s1 = jnp.sum(jnp.where(first, 0.0, sq), axis=-1, keepdims=True)
        r = lax.rsqrt(jnp.where(first, s0, s1) * (1.0 / HEAD_DIM) + EPS)
        outs.append((slab * r * gain * scale).astype(BF16))
    return jnp.concatenate(outs, axis=-1)


def _in_proj_kernel(x_ref, g_ref, w_ref, gq_ref, gk_ref,
                    q_ref, k_ref, v_ref, u_ref, c_ref, gate_ref, h_scr, a_scr):
    j = pl.program_id(1)
    batch, ts, _ = x_ref.shape
    rows = batch * ts

    @pl.when(j == 0)
    def _():
        x = x_ref[...].reshape(rows, D_MODEL)
        h_scr[...] = _rms_rows(x, g_ref[...]).astype(BF16)

    def project(epilogue):
        for c in range(D_MODEL // MXU_DIM):
            cols = slice(c * MXU_DIM, (c + 1) * MXU_DIM)
            epilogue(cols, jnp.dot(h_scr[...], w_ref[:, cols], preferred_element_type=F32))

    def store(ref, fn):
        def epilogue(cols, z):
            val = fn(cols, z)
            ref[:, :, cols] = val.reshape(batch, ts, MXU_DIM)
        return epilogue

    @pl.when(j == 0)
    def _():
        project(store(q_ref, lambda cols, z: _head_norm(z, gq_ref[...], HEAD_DIM ** -0.5 * LOG2E)))

    @pl.when(j == 1)
    def _():
        project(store(k_ref, lambda cols, z: _head_norm(z, gk_ref[...], 1.0)))

    @pl.when(j == 2)
    def _():
        project(store(v_ref, lambda cols, z: z.astype(BF16)))

    def to_strips(cols, z):
        for s in range(MXU_DIM // LANES):
            a_scr[cols.start // LANES + s] = z[:, s * LANES:(s + 1) * LANES]

    def from_strips(cols):
        first = cols.start // LANES
        return jnp.concatenate([a_scr[first + s] for s in range(MXU_DIM // LANES)], axis=1)

    @pl.when(j == 3)
    def _():
        project(to_strips)

        def to_time_major(t, carry):
            u_ref[t] = jnp.concatenate([a_scr[s, pl.ds(t, batch, stride=ts), :]
                                        for s in range(D_MODEL // LANES)], axis=1)
            return carry

        lax.fori_loop(0, ts, to_time_major, 0, unroll=16)

    @pl.when(j == 4)
    def _():
        project(to_strips)

    @pl.when(j == 5)
    def _():
        project(store(c_ref, lambda cols, z: (from_strips(cols) * _sigmoid(z)).astype(BF16)))

    @pl.when(j >= 6)
    def _():
        project(store(gate_ref, lambda cols, z: _sigmoid(z).astype(BF16)))


def _in_proj(x, g, w, gq, gk):
    batch, seq, _ = x.shape
    ts = IN_SEQ_TILE
    rows = batch * ts
    blk = (batch, ts, D_MODEL)
    tile = lambda i, j: (0, i, 0)
    act = jax.ShapeDtypeStruct((batch, seq, D_MODEL), BF16)
    return pl.pallas_call(
        _in_proj_kernel,
        grid=(seq // ts, IN_TILES),
        in_specs=[
            pl.BlockSpec(blk, tile),
            pl.BlockSpec((1, D_MODEL), lambda i, j: (0, 0)),
            pl.BlockSpec((D_MODEL, D_MODEL), lambda i, j: (0, j)),
            pl.BlockSpec((1, HEAD_WIDTH), lambda i, j: (0, 0)),
            pl.BlockSpec((1, HEAD_WIDTH), lambda i, j: (0, 0)),
        ],
        out_specs=[
            pl.BlockSpec(blk, tile),
            pl.BlockSpec(blk, tile),
            pl.BlockSpec(blk, tile),
            pl.BlockSpec((ts, batch, D_MODEL), lambda i, j: (i, 0, 0)),
            pl.BlockSpec(blk, tile),
            pl.BlockSpec(blk, lambda i, j: (0, i, jnp.maximum(j - 6, 0))),
        ],
        out_shape=[
            act, act, act,
            jax.ShapeDtypeStruct((seq, batch, D_MODEL), F32),
            act,
            jax.ShapeDtypeStruct((batch, seq, 3 * D_MODEL), BF16),
        ],
        scratch_shapes=[pltpu.VMEM((rows, D_MODEL), BF16),
                        pltpu.VMEM((D_MODEL // LANES, rows, LANES), F32)],
        compiler_params=_params(("parallel", "arbitrary")),
        name="in_proj",
    )(x, g, w, gq, gk)


def _t5_bucket_ids(rel):
    nb = REL_BUCKETS // 2
    n = -rel
    ret = jnp.where(n < 0, nb, 0)
    n = jnp.abs(n)
    max_exact = nb // 2
    nf = jnp.maximum(n, 1).astype(F32)
    large = max_exact + (jnp.log(nf / max_exact) / math.log(REL_MAX_DIST / max_exact)
                         * (nb - max_exact)).astype(jnp.int32)
    large = jnp.minimum(large, nb - 1)
    return ret + jnp.where(n < max_exact, n, large)


def _bias_kernel(far_ref, tab_ref, ids_ref, out_ref):
    h = pl.program_id(0)
    tq, tk = ATTN_Q, ATTN_K
    ids = ids_ref[...]
    acc = jnp.zeros(ids.shape, F32)
    for b in range(REL_BUCKETS):
        acc = jnp.where(ids == b, tab_ref[b, h], acc)
    acc = (acc - tab_ref[far_ref[0], h]) * LOG2E
    qi = lax.broadcasted_iota(jnp.int32, (tq, tk), 0)
    kj = lax.broadcasted_iota(jnp.int32, (tq, tk), 1)
    out_ref[0] = jnp.zeros((2 * tq, tk), F32)
    for n, o in enumerate(range(-1, ATTN_KQ)):
        visible = ((o + 1) * tk + kj) // CHUNK <= (tk + qi) // CHUNK
        tile = jnp.where(visible, acc[n], NEG_INF)
        out_ref[n + 1, 0:tq] = tile
        out_ref[n + 1, tq:2 * tq] = tile


def _bias_tiles(rel_bias):
    tq, tk = ATTN_Q, ATTN_K
    qi = jnp.arange(tq, dtype=jnp.int32)[:, None]
    kj = jnp.arange(tk, dtype=jnp.int32)[None, :]
    ids = jnp.stack([_t5_bucket_ids(o * tk + kj - qi) for o in range(-1, ATTN_KQ)])
    far = _t5_bucket_ids(jnp.full((1,), -(tk + 1), jnp.int32))
    n_tiles = ATTN_KQ + 2
    return pl.pallas_call(
        _bias_kernel,
        grid=(HEADS,),
        in_specs=[
            pl.BlockSpec(memory_space=pltpu.SMEM),
            pl.BlockSpec(memory_space=pltpu.SMEM),
            pl.BlockSpec((ATTN_KQ + 1, tq, tk), lambda h: (0, 0, 0)),
        ],
        out_specs=pl.BlockSpec((None, n_tiles, 2 * tq, tk), lambda h: (h, 0, 0, 0)),
        out_shape=jax.ShapeDtypeStruct((HEADS, n_tiles, 2 * tq, tk), F32),
        compiler_params=_params(("parallel",)),
        name="bias_tiles",
    )(far, rel_bias, ids)


def _attn_kernel(lamv_ref, subln_ref, bias_ref, q_ref, k_ref, v_ref, o_ref,
                 s_scr, vx_scr, m_scr, acc_scr, *, lam_init):
    tq, tk = ATTN_Q, ATTN_K
    i = pl.program_id(2)
    n_groups = i + 1

    @pl.when(i == 0)
    def _():
        vx_scr[:, 0:HEAD_WIDTH] = v_ref[...]
        vx_scr[:, HEAD_WIDTH:] = jnp.ones((vx_scr.shape[0], HEAD_WIDTH), BF16)

    q = q_ref[...]
    lane = lax.broadcasted_iota(jnp.int32, (1, HEAD_WIDTH), 1)
    first = lane < HEAD_DIM
    zero = jnp.zeros_like(q)
    qs = jnp.concatenate([jnp.where(first, q, zero), jnp.where(first, zero, q)], axis=0)

    def rows(j):
        return pl.ds(pl.multiple_of(j * tk, tk), tk)

    def score_tile(j):
        s = lax.dot_general(qs, k_ref[rows(j), :], (((1,), (1,)), ((), ())),
                            preferred_element_type=F32)
        s = s + bias_ref[jnp.clip(j - ATTN_KQ * i + 2, 0, ATTN_KQ + 1)]
        s_scr[j] = s
        return jnp.maximum(s[:, 0:LANES], s[:, LANES:2 * LANES])

    m_scr[...] = jnp.full(m_scr.shape, NEG_INF, F32)

    def sweep(groups_fn):
        start = 0
        for size in ATTN_GROUPS:
            trips = (n_groups - start) // size

            def body(p, carry, start=start, size=size):
                groups_fn(start + p * size, size)
                return carry

            lax.fori_loop(0, trips, body, 0)
            start = start + trips * size

    def max_groups(g0, count):
        tile_max = [score_tile(g0 * ATTN_KQ + u) for u in range(count * ATTN_KQ)]
        m_scr[...] = functools.reduce(jnp.maximum, tile_max, m_scr[...])

    sweep(max_groups)
    m_scr[...] = jnp.broadcast_to(jnp.max(m_scr[...], axis=-1, keepdims=True), m_scr.shape)
    acc_scr[...] = jnp.zeros(acc_scr.shape, F32)

    def prob_tile(j):
        m_rep = m_scr[...]
        p = jnp.exp2(s_scr[j] - jnp.concatenate([m_rep, m_rep], axis=1))
        return jnp.dot(p.astype(BF16), vx_scr[rows(j), :], preferred_element_type=F32)

    def sum_groups(g0, count):
        parts = [prob_tile(g0 * ATTN_KQ + u) for u in range(count * ATTN_KQ)]
        acc_scr[...] += functools.reduce(lambda a, b: a + b, parts)

    sweep(sum_groups)

    lv = lamv_ref[...]
    lam = (jnp.exp(jnp.sum(lv[0:1] * lv[1:2], axis=-1, keepdims=True))
           - jnp.exp(jnp.sum(lv[2:3] * lv[3:4], axis=-1, keepdims=True)) + lam_init)
    pv, l = acc_scr[:, 0:HEAD_WIDTH], acc_scr[:, HEAD_WIDTH:]
    o = pv[0:tq] / l[0:tq] - lam * (pv[tq:2 * tq] / l[tq:2 * tq])
    o_ref[...] = (_rms_rows(o, subln_ref[...]) * (1.0 - lam_init)).astype(BF16)


def _attention(q, k, v, bias, lamv, subln, lam_init):
    b, s, _ = q.shape
    tq, tk = ATTN_Q, ATTN_K
    return pl.pallas_call(
        functools.partial(_attn_kernel, lam_init=lam_init),
        grid=(b, HEADS, s // tq),
        in_specs=[
            pl.BlockSpec((4, HEAD_DIM), lambda bi, h, i: (0, 0)),
            pl.BlockSpec((1, HEAD_WIDTH), lambda bi, h, i: (0, 0)),
            pl.BlockSpec((None, ATTN_KQ + 2, 2 * tq, tk), lambda bi, h, i: (h, 0, 0, 0)),
            pl.BlockSpec((None, tq, HEAD_WIDTH), lambda bi, h, i: (bi, i, h)),
            pl.BlockSpec((None, s, HEAD_WIDTH), lambda bi, h, i: (bi, 0, h)),
            pl.BlockSpec((None, s, HEAD_WIDTH), lambda bi, h, i: (bi, 0, h)),
        ],
        out_specs=pl.BlockSpec((None, tq, HEAD_WIDTH), lambda bi, h, i: (bi, i, h)),
        out_shape=jax.ShapeDtypeStruct((b, s, D_MODEL), BF16),
        scratch_shapes=[
            pltpu.VMEM((s // tk, 2 * tq, tk), F32),
            pltpu.VMEM((s, 2 * HEAD_WIDTH), BF16),
            pltpu.VMEM((2 * tq, LANES), F32),
            pltpu.VMEM((2 * tq, 2 * HEAD_WIDTH), F32),
        ],
        compiler_params=_params(("parallel", "parallel", "arbitrary")),
        name="diff_attention",
    )(lamv, subln, bias, q, k, v)


def _s5_discretise(lam_re, lam_im, log_step, b_re, b_im, c_re, c_im):
    step = jnp.exp(log_step.astype(F32))[:, None]
    lr, li = lam_re.astype(F32), lam_im.astype(F32)
    mag = jnp.exp(lr * step)
    ab_re, ab_im = mag * jnp.cos(li * step), mag * jnp.sin(li * step)
    den = lr * lr + li * li
    nr, ni = ab_re - 1.0, ab_im
    f_re = (nr * lr + ni * li) / den
    f_im = (ni * lr - nr * li) / den
    br, bi = b_re.astype(F32), b_im.astype(F32)
    bb_re = f_re[..., None] * br - f_im[..., None] * bi
    bb_im = f_re[..., None] * bi + f_im[..., None] * br
    gs = MXU_DIM // S5_GROUP
    state = jnp.arange(S5_SLAB_STATE, dtype=jnp.int32)
    chan = jnp.arange(MXU_DIM, dtype=jnp.int32)
    spread = (state[None, :] % S5_STATE == jnp.arange(S5_STATE, dtype=jnp.int32)[:, None]).astype(BF16)
    own = (chan[:, None] // S5_GROUP) == (state[None, :] // S5_STATE)

    def in_blocks(bb):
        compact = bb.reshape(S5_SLABS, gs, S5_STATE, S5_GROUP).transpose(0, 1, 3, 2)
        compact = compact.reshape(S5_SLABS, MXU_DIM, S5_STATE).astype(BF16)
        blk = jnp.einsum('krp,pc->krc', compact, spread, preferred_element_type=F32)
        return jnp.where(own, blk, 0.0).astype(BF16)

    def out_blocks(c):
        compact = c.reshape(S5_SLABS, MXU_DIM, S5_STATE).astype(BF16)
        blk = jnp.einsum('pc,krp->kcr', spread, compact, preferred_element_type=F32)
        return jnp.where(own.T, blk, 0.0).astype(BF16)

    b_blk = jnp.concatenate([in_blocks(bb_re), in_blocks(bb_im)], axis=2)
    c_blk = jnp.concatenate([out_blocks(c_re.astype(F32)), out_blocks(-c_im.astype(F32))], axis=1)
    a = jnp.stack([ab_re.reshape(S5_SLABS, S5_SLAB_STATE), ab_im.reshape(S5_SLABS, S5_SLAB_STATE)],
                  axis=1)
    return a, b_blk, c_blk


def _s5_kernel(u_ref, a_ref, b_ref, c_ref, d_ref, y_ref, x_scr, st_scr, y_scr, *, batch):
    w = S5_SLAB_STATE

    @pl.when(pl.program_id(0) == 0)
    def _():
        st_scr[...] = jnp.zeros(st_scr.shape, F32)

    u = u_ref[...]
    ub = u.astype(BF16)
    for kt in range(S5_SLABS):
        x_scr[:, kt * 2 * w:(kt + 1) * 2 * w] = jnp.dot(
            ub[:, kt * MXU_DIM:(kt + 1) * MXU_DIM], b_ref[kt], preferred_element_type=F32)

    for kt in range(S5_SLABS):
        re = slice(kt * 2 * w, kt * 2 * w + w)
        im = slice(kt * 2 * w + w, (kt + 1) * 2 * w)
        ar = jnp.broadcast_to(a_ref[kt, 0:1, :], (batch, w))
        ai = jnp.broadcast_to(a_ref[kt, 1:2, :], (batch, w))

        def step(t, carry, re=re, im=im, ar=ar, ai=ai):
            xr, xi = carry
            rows = pl.ds(pl.multiple_of(t * batch, batch), batch)
            nxr = ar * xr - ai * xi + x_scr[rows, re]
            nxi = ar * xi + ai * xr + x_scr[rows, im]
            x_scr[rows, re] = nxr
            x_scr[rows, im] = nxi
            return nxr, nxi

        xr, xi = lax.fori_loop(0, S5_STEPS, step, (st_scr[:, re], st_scr[:, im]), unroll=4)
        st_scr[:, re] = xr
        st_scr[:, im] = xi

    for kt in range(S5_SLABS):
        cols = slice(kt * MXU_DIM, (kt + 1) * MXU_DIM)
        y = jnp.dot(x_scr[:, kt * 2 * w:(kt + 1) * 2 * w].astype(BF16), c_ref[kt],
                    preferred_element_type=F32)
        y = y + d_ref[:, cols] * u[:, cols]
        y = jax.nn.gelu(y)
        for s in range(MXU_DIM // LANES):
            y_scr[kt * (MXU_DIM // LANES) + s] = y[:, s * LANES:(s + 1) * LANES]
    for b in range(batch):
        y_ref[b] = jnp.concatenate([y_scr[s, pl.ds(b, S5_STEPS, stride=batch), :]
                                    for s in range(D_MODEL // LANES)], axis=1).astype(BF16)


def _s5(u_tm, a, b_blk, c_blk, d, batch, seq):
    rows = S5_STEPS * batch
    w = S5_SLAB_STATE
    return pl.pallas_call(
        functools.partial(_s5_kernel, batch=batch),
        grid=(seq // S5_STEPS,),
        in_specs=[
            pl.BlockSpec((rows, D_MODEL), lambda c: (c, 0)),
            pl.BlockSpec((S5_SLABS, 2, w), lambda c: (0, 0, 0)),
            pl.BlockSpec((S5_SLABS, MXU_DIM, 2 * w), lambda c: (0, 0, 0)),
            pl.BlockSpec((S5_SLABS, 2 * w, MXU_DIM), lambda c: (0, 0, 0)),
            pl.BlockSpec((1, D_MODEL), lambda c: (0, 0)),
        ],
        out_specs=pl.BlockSpec((batch, S5_STEPS, D_MODEL), lambda c: (0, c, 0)),
        out_shape=jax.ShapeDtypeStruct((batch, seq, D_MODEL), BF16),
        scratch_shapes=[
            pltpu.VMEM((rows, S5_SLABS * 2 * w), F32),
            pltpu.VMEM((batch, S5_SLABS * 2 * w), F32),
            pltpu.VMEM((D_MODEL // LANES, rows, LANES), F32),
        ],
        compiler_params=_params(("arbitrary",)),
        name="s5_scan",
    )(u_tm, a, b_blk, c_blk, d)


def _conv_kernel(c_ref, halo_ref, w_ref, b_ref, lg_ref, lb_ref, o_ref,
                 xe_scr, xs_scr, wb_scr, y_scr, *, nsb):
    tm = c_ref.shape[0]
    strips = D_MODEL // LANES
    first = (pl.program_id(0) % nsb) == 0
    xe_scr[0:CONV_HALO, :] = jnp.where(first, 0.0, halo_ref[...].astype(F32))
    xe_scr[CONV_HALO:, :] = c_ref[...].astype(F32)
    for b in range(SUBLANES):
        span = tm + CONV_HALO - (SUBLANES if b else 0)
        for cb in range(strips):
            xs_scr[b, cb, 0:span, :] = xe_scr[b:b + span, cb * LANES:(cb + 1) * LANES]
    for cb in range(strips):
        cols = slice(cb * LANES, (cb + 1) * LANES)
        for k in range(CONV_KERNEL):
            wb_scr[k, cb] = jnp.broadcast_to(w_ref[k:k + 1, cols], (SUBLANES, LANES))
        wb_scr[CONV_KERNEL, cb] = jnp.broadcast_to(b_ref[:, cols], (SUBLANES, LANES))
    lead = CONV_HALO - (CONV_KERNEL - 1)
    reps = tm // SUBLANES

    def strip(cb, carry):
        acc = jnp.tile(wb_scr[CONV_KERNEL, cb], (reps, 1))
        for k in range(CONV_KERNEL):
            shift = (lead + k) % SUBLANES
            base = lead + k - shift
            acc = acc + jnp.tile(wb_scr[k, cb], (reps, 1)) * xs_scr[shift, cb, base:base + tm, :]
        y_scr[cb] = acc
        return carry

    lax.fori_loop(0, strips, strip, 0)
    y = jnp.concatenate([y_scr[cb] for cb in range(strips)], axis=1)
    yc = y - jnp.mean(y, axis=-1, keepdims=True)
    yn = yc * lax.rsqrt(jnp.mean(yc * yc, axis=-1, keepdims=True) + EPS) * lg_ref[...] + lb_ref[...]
    o_ref[...] = (yn * jax.nn.sigmoid(yn)).astype(BF16)


def _conv_module(c, w, b, lg, lb, seq):
    n = c.shape[0]
    tm = CONV_TILE
    nsb = seq // tm
    hb = tm // CONV_HALO
    vec = pl.BlockSpec((1, D_MODEL), lambda i: (0, 0))
    return pl.pallas_call(
        functools.partial(_conv_kernel, nsb=nsb),
        grid=(n // tm,),
        in_specs=[
            pl.BlockSpec((tm, D_MODEL), lambda i: (i, 0)),
            pl.BlockSpec((CONV_HALO, D_MODEL), lambda i: (jnp.maximum(i * hb - 1, 0), 0)),
            pl.BlockSpec((CONV_KERNEL, D_MODEL), lambda i: (0, 0)),
            vec, vec, vec,
        ],
        out_specs=pl.BlockSpec((tm, D_MODEL), lambda i: (i, 0)),
        out_shape=jax.ShapeDtypeStruct((n, D_MODEL), BF16),
        scratch_shapes=[pltpu.VMEM((tm + CONV_HALO, D_MODEL), F32),
                        pltpu.VMEM((SUBLANES, D_MODEL // LANES, tm + CONV_HALO, LANES), F32),
                        pltpu.VMEM((CONV_KERNEL + 1, D_MODEL // LANES, SUBLANES, LANES), F32),
                        pltpu.VMEM((D_MODEL // LANES, tm, LANES), F32)],
        compiler_params=_params(("parallel",)),
        name="conv_module",
    )(c, c, w, b, lg, lb)


def _merge_kernel(o_ref, y_ref, c_ref, g_ref, x_ref, wa_ref, w1_ref, w2_ref, wc_ref, wo_ref, out_ref):
    dot = functools.partial(jnp.dot, preferred_element_type=F32)
    y = y_ref[...]
    mix = g_ref[:, 0:D_MODEL].astype(F32) * dot(o_ref[...], wa_ref[...])
    mix = mix + g_ref[:, D_MODEL:2 * D_MODEL].astype(F32) * (
        dot(y, w1_ref[...]) * jax.nn.sigmoid(dot(y, w2_ref[...])))
    mix = mix + g_ref[:, 2 * D_MODEL:].astype(F32) * dot(c_ref[...], wc_ref[...])
    out_ref[...] = x_ref[...] + dot(mix.astype(BF16), wo_ref[...])


def _merge(o, y, c, gates, x2d, wa, w1, w2, wc, wo):
    n = x2d.shape[0]
    tm = ROW_TILE
    row = pl.BlockSpec((tm, D_MODEL), lambda i: (i, 0))
    wgt = pl.BlockSpec((D_MODEL, D_MODEL), lambda i: (0, 0))
    return pl.pallas_call(
        _merge_kernel,
        grid=(n // tm,),
        in_specs=[
            row, row, row,
            pl.BlockSpec((tm, 3 * D_MODEL), lambda i: (i, 0)),
            row,
            wgt, wgt, wgt, wgt, wgt,
        ],
        out_specs=row,
        out_shape=jax.ShapeDtypeStruct((n, D_MODEL), F32),
        compiler_params=_params(("parallel",)),
        name="merge",
    )(o, y, c, gates, x2d, wa, w1, w2, wc, wo)


def _ffn_kernel(x_ref, xh_ref, g_ref, wu_ref, dw_ref, wd_ref, out_ref,
                h_scr, acc_scr, *up_scrs, nsb):
    tm = x_ref.shape[0]
    first = (pl.program_id(0) % nsb) == 0
    g = g_ref[...]
    h_scr[0:FFN_HALO, :] = jnp.where(first, 0.0, _rms_rows(xh_ref[...], g)).astype(BF16)
    h_scr[FFN_HALO:, :] = _rms_rows(x_ref[...], g).astype(BF16)
    lead = FFN_HALO - (FFN_KERNEL - 1)

    def taps(scr, dw):
        out = dw[0:1, :] * scr[lead:lead + tm, :]
        for k in range(1, FFN_KERNEL):
            out = out + dw[k:k + 1, :] * scr[lead + k:lead + k + tm, :]
        return out

    def halves(j):
        return (slice(j * FFN_TILE, (j + 1) * FFN_TILE),
                slice(FFN_HIDDEN + j * FFN_TILE, FFN_HIDDEN + (j + 1) * FFN_TILE))

    def up_project(j):
        lin, gate = halves(j)
        h = h_scr[...]
        up_scrs[2 * (j % 2)][...] = jnp.dot(h, wu_ref[:, lin], preferred_element_type=F32)
        up_scrs[2 * (j % 2) + 1][...] = jnp.dot(h, wu_ref[:, gate], preferred_element_type=F32)

    up_project(0)
    for j in range(FFN_TILES):
        lin, gate = halves(j)
        if j + 1 < FFN_TILES:
            up_project(j + 1)
        ua_scr, ub_scr = up_scrs[2 * (j % 2)], up_scrs[2 * (j % 2) + 1]
        act = jax.nn.gelu(taps(ub_scr, dw_ref[:, gate])) * taps(ua_scr, dw_ref[:, lin])
        part = jnp.dot(act.astype(BF16), wd_ref[lin, :], preferred_element_type=F32)
        if j == 0:
            acc_scr[...] = part
        else:
            acc_scr[...] += part
    out_ref[...] = x_ref[...] + acc_scr[...]


def _ffn(x2d, g, wu, dw, wd, seq):
    n = x2d.shape[0]
    tm = ROW_TILE
    nsb = seq // tm
    hb = tm // FFN_HALO
    const = lambda i: (0, 0)
    return pl.pallas_call(
        functools.partial(_ffn_kernel, nsb=nsb),
        grid=(n // tm,),
        in_specs=[
            pl.BlockSpec((tm, D_MODEL), lambda i: (i, 0)),
            pl.BlockSpec((FFN_HALO, D_MODEL), lambda i: (jnp.maximum(i * hb - 1, 0), 0)),
            pl.BlockSpec((1, D_MODEL), lambda i: (0, 0)),
            pl.BlockSpec((D_MODEL, 2 * FFN_HIDDEN), const),
            pl.BlockSpec((FFN_KERNEL, 2 * FFN_HIDDEN), const),
            pl.BlockSpec((FFN_HIDDEN, D_MODEL), const),
        ],
        out_specs=pl.BlockSpec((tm, D_MODEL), lambda i: (i, 0)),
        out_shape=jax.ShapeDtypeStruct((n, D_MODEL), F32),
        scratch_shapes=[
            pltpu.VMEM((tm + FFN_HALO, D_MODEL), BF16),
            pltpu.VMEM((tm, D_MODEL), F32),
        ] + [pltpu.VMEM((tm + FFN_HALO, FFN_TILE), F32)] * 4 + [
        ],
        compiler_params=_params(("parallel",)),
        name="ffn",
    )(x2d, x2d, g, wu, dw, wd)


def kernel(x, norm_mix, w_in, qk_gain_q, qk_gain_k, lambda_q1, lambda_k1, lambda_q2, lambda_k2, diff_subln, rel_bias, w_attn_out, s5_lambda_re, s5_lambda_im, s5_log_step, s5_b_re, s5_b_im, s5_c_re, s5_c_im, s5_d, s5_glu_w1, s5_glu_w2, conv_dw_w, conv_dw_b, conv_ln_g, conv_ln_b, conv_w_out, w_out, norm_ffn, ffn_w_up, ffn_dw_w, ffn_w_down):
    batch, seq, _ = x.shape
    assert batch == SUBLANES, "the S5 scan keeps one batch entry per sublane"
    n = batch * seq
    row = lambda p: p.reshape(1, -1).astype(F32)
    bias = _bias_tiles(rel_bias.astype(F32))
    x2d = x.reshape(n, D_MODEL)
    for l in range(DEPTH):
        lam_init = 0.8 - 0.6 * math.exp(-0.3 * l)
        q, k, v, u_tm, c_glu, gates = _in_proj(
            x2d.reshape(batch, seq, D_MODEL), row(norm_mix[l]), w_in[l].astype(BF16),
            row(jnp.tile(qk_gain_q[l], 2)), row(jnp.tile(qk_gain_k[l], 2)))

        lamv = jnp.stack([lambda_q1[l], lambda_k1[l], lambda_q2[l], lambda_k2[l]]).astype(F32)
        o = _attention(q, k, v, bias, lamv, row(diff_subln[l]), lam_init)

        a, b_blk, c_blk = _s5_discretise(s5_lambda_re[l], s5_lambda_im[l], s5_log_step[l],
                                         s5_b_re[l], s5_b_im[l], s5_c_re[l], s5_c_im[l])
        y = _s5(u_tm.reshape(seq * batch, D_MODEL), a, b_blk, c_blk, row(s5_d[l]), batch, seq)

        c = _conv_module(c_glu.reshape(n, D_MODEL), conv_dw_w[l].astype(F32), row(conv_dw_b[l]),
                         row(conv_ln_g[l]), row(conv_ln_b[l]), seq)

        x2d = _merge(o.reshape(n, D_MODEL), y.reshape(n, D_MODEL), c,
                     gates.reshape(n, 3 * D_MODEL), x2d,
                     w_attn_out[l].astype(BF16), s5_glu_w1[l].astype(BF16),
                     s5_glu_w2[l].astype(BF16), conv_w_out[l].astype(BF16), w_out[l].astype(BF16))

        x2d = _ffn(x2d, row(norm_ffn[l]), ffn_w_up[l].astype(BF16), ffn_dw_w[l].astype(F32),
                   ffn_w_down[l].astype(BF16), seq)
    return x2d.reshape(batch, seq, D_MODEL)
```

```python
import functools
import math

import jax
import jax.numpy as jnp
from jax import lax
from jax.experimental import pallas as pl
from jax.experimental.pallas import tpu as pltpu

F32 = jnp.float32
BF16 = jnp.bfloat16

D_MODEL = 1024
DEPTH = 2
CHUNK = 64
HEADS = 8
HEAD_DIM = 64
HEAD_WIDTH = 2 * HEAD_DIM
S5_GROUP = 16
S5_GROUPS = 64
S5_STATE = 64
CONV_KERNEL = 31
FFN_HIDDEN = 2816
FFN_KERNEL = 3
REL_BUCKETS = 32
REL_MAX_DIST = 128
EPS = 1e-6
IN_TILES = 9
NEG_INF = -1e30

LANES = 128
SUBLANES = 8
MXU_DIM = 256
VMEM_LIMIT = 56 * 1024 * 1024

ROW_TILE = 512
IN_SEQ_TILE = 128
ATTN_Q = 512
ATTN_K = 256
ATTN_KQ = ATTN_Q // ATTN_K
ATTN_GROUPS = (4, 2, 1)
LOG2E = math.log2(math.e)
S5_STEPS = 64
S5_SLABS = D_MODEL // MXU_DIM
S5_SLAB_STATE = (MXU_DIM // S5_GROUP) * S5_STATE
CONV_HALO = 32
CONV_TILE = 256
FFN_HALO = 16
FFN_TILE = 256
FFN_TILES = FFN_HIDDEN // FFN_TILE


def _params(semantics):
    return pltpu.CompilerParams(dimension_semantics=semantics, vmem_limit_bytes=VMEM_LIMIT)


def _sigmoid(x):
    return 0.5 * jnp.tanh(0.5 * x) + 0.5


def _rms_rows(x, g):
    return x * lax.rsqrt(jnp.mean(x * x, axis=-1, keepdims=True) + EPS) * g


def _head_norm(z, gain, scale):
    lane = lax.broadcasted_iota(jnp.int32, (1, HEAD_WIDTH), 1)
    first = lane < HEAD_DIM
    outs = []
    for h in range(z.shape[1] // HEAD_WIDTH):
        slab = z[:, h * HEAD_WIDTH:(h + 1) * HEAD_WIDTH]
        sq = slab * slab
        s0 = jnp.sum(jnp.where(first, sq, 0.0), axis=-1, keepdims=True)
        s1 = jnp.sum(jnp.where(first, 0.0, sq), axis=-1, keepdims=True)
        r = lax.rsqrt(jnp.where(first, s0, s1) * (1.0 / HEAD_DIM) + EPS)
        outs.append((slab * r * gain * scale).astype(BF16))
    return jnp.concatenate(outs, axis=-1)


def _in_proj_kernel(x_ref, g_ref, w_ref, gq_ref, gk_ref,
                    q_ref, k_ref, v_ref, u_ref, c_ref, gate_ref, h_scr, a_scr):
    j = pl.program_id(1)
    batch, ts, _ = x_ref.shape
    rows = batch * ts

    @pl.when(j == 0)
    def _():
        x = x_ref[...].reshape(rows, D_MODEL)
        h_scr[...] = _rms_rows(x, g_ref[...]).astype(BF16)

    def project(epilogue):
        for c in range(D_MODEL // MXU_DIM):
            cols = slice(c * MXU_DIM, (c + 1) * MXU_DIM)
            epilogue(cols, jnp.dot(h_scr[...], w_ref[:, cols], preferred_element_type=F32))

    def store(ref, fn):
        def epilogue(cols, z):
            val = fn(cols, z)
            ref[:, :, cols] = val.reshape(batch, ts, MXU_DIM)
        return epilogue

    @pl.when(j == 0)
    def _():
        project(store(q_ref, lambda cols, z: _head_norm(z, gq_ref[...], HEAD_DIM ** -0.5 * LOG2E)))

    @pl.when(j == 1)
    def _():
        project(store(k_ref, lambda cols, z: _head_norm(z, gk_ref[...], 1.0)))

    @pl.when(j == 2)
    def _():
        project(store(v_ref, lambda cols, z: z.astype(BF16)))

    def to_strips(cols, z):
        for s in range(MXU_DIM // LANES):
            a_scr[cols.start // LANES + s] = z[:, s * LANES:(s + 1) * LANES]

    def from_strips(cols):
        first = cols.start // LANES
        return jnp.concatenate([a_scr[first + s] for s in range(MXU_DIM // LANES)], axis=1)

    @pl.when(j == 3)
    def _():
        project(to_strips)

        def to_time_major(t, carry):
            u_ref[t] = jnp.concatenate([a_scr[s, pl.ds(t, batch, stride=ts), :]
                                        for s in range(D_MODEL // LANES)], axis=1)
            return carry

        lax.fori_loop(0, ts, to_time_major, 0, unroll=16)

    @pl.when(j == 4)
    def _():
        project(to_strips)

    @pl.when(j == 5)
    def _():
        project(store(c_ref, lambda cols, z: (from_strips(cols) * _sigmoid(z)).astype(BF16)))

    @pl.when(j >= 6)
    def _():
        project(store(gate_ref, lambda cols, z: _sigmoid(z).astype(BF16)))


def _in_proj(x, g, w, gq, gk):
    batch, seq, _ = x.shape
    ts = IN_SEQ_TILE
    rows = batch * ts
    blk = (batch, ts, D_MODEL)
    tile = lambda i, j: (0, i, 0)
    act = jax.ShapeDtypeStruct((batch, seq, D_MODEL), BF16)
    return pl.pallas_call(
        _in_proj_kernel,
        grid=(seq // ts, IN_TILES),
        in_specs=[
            pl.BlockSpec(blk, tile),
            pl.BlockSpec((1, D_MODEL), lambda i, j: (0, 0)),
            pl.BlockSpec((D_MODEL, D_MODEL), lambda i, j: (0, j)),
            pl.BlockSpec((1, HEAD_WIDTH), lambda i, j: (0, 0)),
            pl.BlockSpec((1, HEAD_WIDTH), lambda i, j: (0, 0)),
        ],
        out_specs=[
            pl.BlockSpec(blk, tile),
            pl.BlockSpec(blk, tile),
            pl.BlockSpec(blk, tile),
            pl.BlockSpec((ts, batch, D_MODEL), lambda i, j: (i, 0, 0)),
            pl.BlockSpec(blk, tile),
            pl.BlockSpec(blk, lambda i, j: (0, i, jnp.maximum(j - 6, 0))),
        ],
        out_shape=[
            act, act, act,
            jax.ShapeDtypeStruct((seq, batch, D_MODEL), F32),
            act,
            jax.ShapeDtypeStruct((batch, seq, 3 * D_MODEL), BF16),
        ],
        scratch_shapes=[pltpu.VMEM((rows, D_MODEL), BF16),
                        pltpu.VMEM((D_MODEL // LANES, rows, LANES), F32)],
        compiler_params=_params(("parallel", "arbitrary")),
        name="in_proj",
    )(x, g, w, gq, gk)


def _t5_bucket_ids(rel):
    nb = REL_BUCKETS // 2
    n = -rel
    ret = jnp.where(n < 0, nb, 0)
    n = jnp.abs(n)
    max_exact = nb // 2
    nf = jnp.maximum(n, 1).astype(F32)
    large = max_exact + (jnp.log(nf / max_exact) / math.log(REL_MAX_DIST / max_exact)
                         * (nb - max_exact)).astype(jnp.int32)
    large = jnp.minimum(large, nb - 1)
    return ret + jnp.where(n < max_exact, n, large)


def _bias_kernel(far_ref, tab_ref, ids_ref, out_ref):
    h = pl.program_id(0)
    tq, tk = ATTN_Q, ATTN_K
    ids = ids_ref[...]
    acc = jnp.zeros(ids.shape, F32)
    for b in range(REL_BUCKETS):
        acc = jnp.where(ids == b, tab_ref[b, h], acc)
    acc = (acc - tab_ref[far_ref[0], h]) * LOG2E
    qi = lax.broadcasted_iota(jnp.int32, (tq, tk), 0)
    kj = lax.broadcasted_iota(jnp.int32, (tq, tk), 1)
    out_ref[0] = jnp.zeros((2 * tq, tk), F32)
    for n, o in enumerate(range(-1, ATTN_KQ)):
        visible = ((o + 1) * tk + kj) // CHUNK <= (tk + qi) // CHUNK
        tile = jnp.where(visible, acc[n], NEG_INF)
        out_ref[n + 1, 0:tq] = tile
        out_ref[n + 1, tq:2 * tq] = tile


def _bias_tiles(rel_bias):
    tq, tk = ATTN_Q, ATTN_K
    qi = jnp.arange(tq, dtype=jnp.int32)[:, None]
    kj = jnp.arange(tk, dtype=jnp.int32)[None, :]
    ids = jnp.stack([_t5_bucket_ids(o * tk + kj - qi) for o in range(-1, ATTN_KQ)])
    far = _t5_bucket_ids(jnp.full((1,), -(tk + 1), jnp.int32))
    n_tiles = ATTN_KQ + 2
    return pl.pallas_call(
        _bias_kernel,
        grid=(HEADS,),
        in_specs=[
            pl.BlockSpec(memory_space=pltpu.SMEM),
            pl.BlockSpec(memory_space=pltpu.SMEM),
            pl.BlockSpec((ATTN_KQ + 1, tq, tk), lambda h: (0, 0, 0)),
        ],
        out_specs=pl.BlockSpec((None, n_tiles, 2 * tq, tk), lambda h: (h, 0, 0, 0)),
        out_shape=jax.ShapeDtypeStruct((HEADS, n_tiles, 2 * tq, tk), F32),
        compiler_params=_params(("parallel",)),
        name="bias_tiles",
    )(far, rel_bias, ids)


def _attn_kernel(lamv_ref, subln_ref, bias_ref, q_ref, k_ref, v_ref, o_ref,
                 s_scr, vx_scr, m_scr, acc_scr, *, lam_init):
    tq, tk = ATTN_Q, ATTN_K
    i = pl.program_id(2)
    n_groups = i + 1

    @pl.when(i == 0)
    def _():
        vx_scr[:, 0:HEAD_WIDTH] = v_ref[...]
        vx_scr[:, HEAD_WIDTH:] = jnp.ones((vx_scr.shape[0], HEAD_WIDTH), BF16)

    q = q_ref[...]
    lane = lax.broadcasted_iota(jnp.int32, (1, HEAD_WIDTH), 1)
    first = lane < HEAD_DIM
    zero = jnp.zeros_like(q)
    qs = jnp.concatenate([jnp.where(first, q, zero), jnp.where(first, zero, q)], axis=0)

    def rows(j):
        return pl.ds(pl.multiple_of(j * tk, tk), tk)

    def score_tile(j):
        s = lax.dot_general(qs, k_ref[rows(j), :], (((1,), (1,)), ((), ())),
                            preferred_element_type=F32)
        s = s + bias_ref[jnp.clip(j - ATTN_KQ * i + 2, 0, ATTN_KQ + 1)]
        s_scr[j] = s
        return jnp.maximum(s[:, 0:LANES], s[:, LANES:2 * LANES])

    m_scr[...] = jnp.full(m_scr.shape, NEG_INF, F32)

    def sweep(groups_fn):
        start = 0
        for size in ATTN_GROUPS:
            trips = (n_groups - start) // size

            def body(p, carry, start=start, size=size):
                groups_fn(start + p * size, size)
                return carry

            lax.fori_loop(0, trips, body, 0)
            start = start + trips * size

    def max_groups(g0, count):
        tile_max = [score_tile(g0 * ATTN_KQ + u) for u in range(count * ATTN_KQ)]
        m_scr[...] = functools.reduce(jnp.maximum, tile_max, m_scr[...])

    sweep(max_groups)
    m_scr[...] = jnp.broadcast_to(jnp.max(m_scr[...], axis=-1, keepdims=True), m_scr.shape)
    acc_scr[...] = jnp.zeros(acc_scr.shape, F32)

    def prob_tile(j):
        m_rep = m_scr[...]
        p = jnp.exp2(s_scr[j] - jnp.concatenate([m_rep, m_rep], axis=1))
        return jnp.dot(p.astype(BF16), vx_scr[rows(j), :], preferred_element_type=F32)

    def sum_groups(g0, count):
        parts = [prob_tile(g0 * ATTN_KQ + u) for u in range(count * ATTN_KQ)]
        acc_scr[...] += functools.reduce(lambda a, b: a + b, parts)

    sweep(sum_groups)

    lv = lamv_ref[...]
    lam = (jnp.exp(jnp.sum(lv[0:1] * lv[1:2], axis=-1, keepdims=True))
           - jnp.exp(jnp.sum(lv[2:3] * lv[3:4], axis=-1, keepdims=True)) + lam_init)
    pv, l = acc_scr[:, 0:HEAD_WIDTH], acc_scr[:, HEAD_WIDTH:]
    o = pv[0:tq] / l[0:tq] - lam * (pv[tq:2 * tq] / l[tq:2 * tq])
    o_ref[...] = (_rms_rows(o, subln_ref[...]) * (1.0 - lam_init)).astype(BF16)


def _attention(q, k, v, bias, lamv, subln, lam_init):
    b, s, _ = q.shape
    tq, tk = ATTN_Q, ATTN_K
    return pl.pallas_call(
        functools.partial(_attn_kernel, lam_init=lam_init),
        grid=(b, HEADS, s // tq),
        in_specs=[
            pl.BlockSpec((4, HEAD_DIM), lambda bi, h, i: (0, 0)),
            pl.BlockSpec((1, HEAD_WIDTH), lambda bi, h, i: (0, 0)),
            pl.BlockSpec((None, ATTN_KQ + 2, 2 * tq, tk), lambda bi, h, i: (h, 0, 0, 0)),
            pl.BlockSpec((None, tq, HEAD_WIDTH), lambda bi, h, i: (bi, i, h)),
            pl.BlockSpec((None, s, HEAD_WIDTH), lambda bi, h, i: (bi, 0, h)),
            pl.BlockSpec((None, s, HEAD_WIDTH), lambda bi, h, i: (bi, 0, h)),
        ],
        out_specs=pl.BlockSpec((None, tq, HEAD_WIDTH), lambda bi, h, i: (bi, i, h)),
        out_shape=jax.ShapeDtypeStruct((b, s, D_MODEL), BF16),
        scratch_shapes=[
            pltpu.VMEM((s // tk, 2 * tq, tk), F32),
            pltpu.VMEM((s, 2 * HEAD_WIDTH), BF16),
            pltpu.VMEM((2 * tq, LANES), F32),
            pltpu.VMEM((2 * tq, 2 * HEAD_WIDTH), F32),
        ],
        compiler_params=_params(("parallel", "parallel", "arbitrary")),
        name="diff_attention",
    )(lamv, subln, bias, q, k, v)


def _s5_discretise(lam_re, lam_im, log_step, b_re, b_im, c_re, c_im):
    step = jnp.exp(log_step.astype(F32))[:, None]
    lr, li = lam_re.astype(F32), lam_im.astype(F32)
    mag = jnp.exp(lr * step)
    ab_re, ab_im = mag * jnp.cos(li * step), mag * jnp.sin(li * step)
    den = lr * lr + li * li
    nr, ni = ab_re - 1.0, ab_im
    f_re = (nr * lr + ni * li) / den
    f_im = (ni * lr - nr * li) / den
    br, bi = b_re.astype(F32), b_im.astype(F32)
    bb_re = f_re[..., None] * br - f_im[..., None] * bi
    bb_im = f_re[..., None] * bi + f_im[..., None] * br
    gs = MXU_DIM // S5_GROUP
    state = jnp.arange(S5_SLAB_STATE, dtype=jnp.int32)
    chan = jnp.arange(MXU_DIM, dtype=jnp.int32)
    spread = (state[None, :] % S5_STATE == jnp.arange(S5_STATE, dtype=jnp.int32)[:, None]).astype(BF16)
    own = (chan[:, None] // S5_GROUP) == (state[None, :] // S5_STATE)

    def in_blocks(bb):
        compact = bb.reshape(S5_SLABS, gs, S5_STATE, S5_GROUP).transpose(0, 1, 3, 2)
        compact = compact.reshape(S5_SLABS, MXU_DIM, S5_STATE).astype(BF16)
        blk = jnp.einsum('krp,pc->krc', compact, spread, preferred_element_type=F32)
        return jnp.where(own, blk, 0.0).astype(BF16)

    def out_blocks(c):
        compact = c.reshape(S5_SLABS, MXU_DIM, S5_STATE).astype(BF16)
        blk = jnp.einsum('pc,krp->kcr', spread, compact, preferred_element_type=F32)
        return jnp.where(own.T, blk, 0.0).astype(BF16)

    b_blk = jnp.concatenate([in_blocks(bb_re), in_blocks(bb_im)], axis=2)
    c_blk = jnp.concatenate([out_blocks(c_re.astype(F32)), out_blocks(-c_im.astype(F32))], axis=1)
    a = jnp.stack([ab_re.reshape(S5_SLABS, S5_SLAB_STATE), ab_im.reshape(S5_SLABS, S5_SLAB_STATE)],
                  axis=1)
    return a, b_blk, c_blk


def _s5_kernel(u_ref, a_ref, b_ref, c_ref, d_ref, y_ref, x_scr, st_scr, y_scr, *, batch):
    w = S5_SLAB_STATE

    @pl.when(pl.program_id(0) == 0)
    def _():
        st_scr[...] = jnp.zeros(st_scr.shape, F32)

    u = u_ref[...]
    ub = u.astype(BF16)
    for kt in range(S5_SLABS):
        x_scr[:, kt * 2 * w:(kt + 1) * 2 * w] = jnp.dot(
            ub[:, kt * MXU_DIM:(kt + 1) * MXU_DIM], b_ref[kt], preferred_element_type=F32)

    for kt in range(S5_SLABS):
        re = slice(kt * 2 * w, kt * 2 * w + w)
        im = slice(kt * 2 * w + w, (kt + 1) * 2 * w)
        ar = jnp.broadcast_to(a_ref[kt, 0:1, :], (batch, w))
        ai = jnp.broadcast_to(a_ref[kt, 1:2, :], (batch, w))

        def step(t, carry, re=re, im=im, ar=ar, ai=ai):
            xr, xi = carry
            rows = pl.ds(pl.multiple_of(t * batch, batch), batch)
            nxr = ar * xr - ai * xi + x_scr[rows, re]
            nxi = ar * xi + ai * xr + x_scr[rows, im]
            x_scr[rows, re] = nxr
            x_scr[rows, im] = nxi
            return nxr, nxi

        xr, xi = lax.fori_loop(0, S5_STEPS, step, (st_scr[:, re], st_scr[:, im]), unroll=4)
        st_scr[:, re] = xr
        st_scr[:, im] = xi

    for kt in range(S5_SLABS):
        cols = slice(kt * MXU_DIM, (kt + 1) * MXU_DIM)
        y = jnp.dot(x_scr[:, kt * 2 * w:(kt + 1) * 2 * w].astype(BF16), c_ref[kt],
                    preferred_element_type=F32)
        y = y + d_ref[:, cols] * u[:, cols]
        y = jax.nn.gelu(y)
        for s in range(MXU_DIM // LANES):
            y_scr[kt * (MXU_DIM // LANES) + s] = y[:, s * LANES:(s + 1) * LANES]
    for b in range(batch):
        y_ref[b] = jnp.concatenate([y_scr[s, pl.ds(b, S5_STEPS, stride=batch), :]
                                    for s in range(D_MODEL // LANES)], axis=1).astype(BF16)


def _s5(u_tm, a, b_blk, c_blk, d, batch, seq):
    rows = S5_STEPS * batch
    w = S5_SLAB_STATE
    return pl.pallas_call(
        functools.partial(_s5_kernel, batch=batch),
        grid=(seq // S5_STEPS,),
        in_specs=[
            pl.BlockSpec((rows, D_MODEL), lambda c: (c, 0)),
            pl.BlockSpec((S5_SLABS, 2, w), lambda c: (0, 0, 0)),
            pl.BlockSpec((S5_SLABS, MXU_DIM, 2 * w), lambda c: (0, 0, 0)),
            pl.BlockSpec((S5_SLABS, 2 * w, MXU_DIM), lambda c: (0, 0, 0)),
            pl.BlockSpec((1, D_MODEL), lambda c: (0, 0)),
        ],
        out_specs=pl.BlockSpec((batch, S5_STEPS, D_MODEL), lambda c: (0, c, 0)),
        out_shape=jax.ShapeDtypeStruct((batch, seq, D_MODEL), BF16),
        scratch_shapes=[
            pltpu.VMEM((rows, S5_SLABS * 2 * w), F32),
            pltpu.VMEM((batch, S5_SLABS * 2 * w), F32),
            pltpu.VMEM((D_MODEL // LANES, rows, LANES), F32),
        ],
        compiler_params=_params(("arbitrary",)),
        name="s5_scan",
    )(u_tm, a, b_blk, c_blk, d)


def _conv_kernel(c_ref, halo_ref, w_ref, b_ref, lg_ref, lb_ref, o_ref,
                 xe_scr, xs_scr, wb_scr, y_scr, *, nsb):
    tm = c_ref.shape[0]
    strips = D_MODEL // LANES
    first = (pl.program_id(0) % nsb) == 0
    xe_scr[0:CONV_HALO, :] = jnp.where(first, 0.0, halo_ref[...].astype(F32))
    xe_scr[CONV_HALO:, :] = c_ref[...].astype(F32)
    for b in range(SUBLANES):
        span = tm + CONV_HALO - (SUBLANES if b else 0)
        for cb in range(strips):
            xs_scr[b, cb, 0:span, :] = xe_scr[b:b + span, cb * LANES:(cb + 1) * LANES]
    for cb in range(strips):
        cols = slice(cb * LANES, (cb + 1) * LANES)
        for k in range(CONV_KERNEL):
            wb_scr[k, cb] = jnp.broadcast_to(w_ref[k:k + 1, cols], (SUBLANES, LANES))
        wb_scr[CONV_KERNEL, cb] = jnp.broadcast_to(b_ref[:, cols], (SUBLANES, LANES))
    lead = CONV_HALO - (CONV_KERNEL - 1)
    reps = tm // SUBLANES

    def strip(cb, carry):
        acc = jnp.tile(wb_scr[CONV_KERNEL, cb], (reps, 1))
        for k in range(CONV_KERNEL):
            shift = (lead + k) % SUBLANES
            base = lead + k - shift
            acc = acc + jnp.tile(wb_scr[k, cb], (reps, 1)) * xs_scr[shift, cb, base:base + tm, :]
        y_scr[cb] = acc
        return carry

    lax.fori_loop(0, strips, strip, 0)
    y = jnp.concatenate([y_scr[cb] for cb in range(strips)], axis=1)
    yc = y - jnp.mean(y, axis=-1, keepdims=True)
    yn = yc * lax.rsqrt(jnp.mean(yc * yc, axis=-1, keepdims=True) + EPS) * lg_ref[...] + lb_ref[...]
    o_ref[...] = (yn * jax.nn.sigmoid(yn)).astype(BF16)


def _conv_module(c, w, b, lg, lb, seq):
    n = c.shape[0]
    tm = CONV_TILE
    nsb = seq // tm
    hb = tm // CONV_HALO
    vec = pl.BlockSpec((1, D_MODEL), lambda i: (0, 0))
    return pl.pallas_call(
        functools.partial(_conv_kernel, nsb=nsb),
        grid=(n // tm,),
        in_specs=[
            pl.BlockSpec((tm, D_MODEL), lambda i: (i, 0)),
            pl.BlockSpec((CONV_HALO, D_MODEL), lambda i: (jnp.maximum(i * hb - 1, 0), 0)),
            pl.BlockSpec((CONV_KERNEL, D_MODEL), lambda i: (0, 0)),
            vec, vec, vec,
        ],
        out_specs=pl.BlockSpec((tm, D_MODEL), lambda i: (i, 0)),
        out_shape=jax.ShapeDtypeStruct((n, D_MODEL), BF16),
        scratch_shapes=[pltpu.VMEM((tm + CONV_HALO, D_MODEL), F32),
                        pltpu.VMEM((SUBLANES, D_MODEL // LANES, tm + CONV_HALO, LANES), F32),
                        pltpu.VMEM((CONV_KERNEL + 1, D_MODEL // LANES, SUBLANES, LANES), F32),
                        pltpu.VMEM((D_MODEL // LANES, tm, LANES), F32)],
        compiler_params=_params(("parallel",)),
        name="conv_module",
    )(c, c, w, b, lg, lb)


def _merge_kernel(o_ref, y_ref, c_ref, g_ref, x_ref, wa_ref, w1_ref, w2_ref, wc_ref, wo_ref, out_ref):
    dot = functools.partial(jnp.dot, preferred_element_type=F32)
    y = y_ref[...]
    mix = g_ref[:, 0:D_MODEL].astype(F32) * dot(o_ref[...], wa_ref[...])
    mix = mix + g_ref[:, D_MODEL:2 * D_MODEL].astype(F32) * (
        dot(y, w1_ref[...]) * jax.nn.sigmoid(dot(y, w2_ref[...])))
    mix = mix + g_ref[:, 2 * D_MODEL:].astype(F32) * dot(c_ref[...], wc_ref[...])
    out_ref[...] = x_ref[...] + dot(mix.astype(BF16), wo_ref[...])


def _merge(o, y, c, gates, x2d, wa, w1, w2, wc, wo):
    n = x2d.shape[0]
    tm = ROW_TILE
    row = pl.BlockSpec((tm, D_MODEL), lambda i: (i, 0))
    wgt = pl.BlockSpec((D_MODEL, D_MODEL), lambda i: (0, 0))
    return pl.pallas_call(
        _merge_kernel,
        grid=(n // tm,),
        in_specs=[
            row, row, row,
            pl.BlockSpec((tm, 3 * D_MODEL), lambda i: (i, 0)),
            row,
            wgt, wgt, wgt, wgt, wgt,
        ],
        out_specs=row,
        out_shape=jax.ShapeDtypeStruct((n, D_MODEL), F32),
        compiler_params=_params(("parallel",)),
        name="merge",
    )(o, y, c, gates, x2d, wa, w1, w2, wc, wo)


def _ffn_kernel(x_ref, xh_ref, g_ref, wu_ref, dw_ref, wd_ref, out_ref,
                h_scr, acc_scr, *slots, nsb):
    up_scrs, act_scrs = slots[:4], slots[4:]
    tm = x_ref.shape[0]
    first = (pl.program_id(0) % nsb) == 0
    g = g_ref[...]
    h_scr[0:FFN_HALO, :] = jnp.where(first, 0.0, _rms_rows(xh_ref[...], g)).astype(BF16)
    h_scr[FFN_HALO:, :] = _rms_rows(x_ref[...], g).astype(BF16)
    lead = FFN_HALO - (FFN_KERNEL - 1)

    def taps(scr, dw):
        out = dw[0:1, :] * scr[lead:lead + tm, :]
        for k in range(1, FFN_KERNEL):
            out = out + dw[k:k + 1, :] * scr[lead + k:lead + k + tm, :]
        return out

    def halves(j):
        return (slice(j * FFN_TILE, (j + 1) * FFN_TILE),
                slice(FFN_HIDDEN + j * FFN_TILE, FFN_HIDDEN + (j + 1) * FFN_TILE))

    def up_project(j):
        lin, gate = halves(j)
        h = h_scr[...]
        up_scrs[2 * (j % 2)][...] = jnp.dot(h, wu_ref[:, lin], preferred_element_type=F32)
        up_scrs[2 * (j % 2) + 1][...] = jnp.dot(h, wu_ref[:, gate], preferred_element_type=F32)

    def down_project(j):
        lin, _ = halves(j)
        part = jnp.dot(act_scrs[j % 2][...], wd_ref[lin, :], preferred_element_type=F32)
        if j == 0:
            acc_scr[...] = part
        else:
            acc_scr[...] += part

    up_project(0)
    for j in range(FFN_TILES):
        lin, gate = halves(j)
        if j + 1 < FFN_TILES:
            up_project(j + 1)
        ua_scr, ub_scr = up_scrs[2 * (j % 2)], up_scrs[2 * (j % 2) + 1]
        act = jax.nn.gelu(taps(ub_scr, dw_ref[:, gate])) * taps(ua_scr, dw_ref[:, lin])
        act_scrs[j % 2][...] = act.astype(BF16)
        if j >= 1:
            down_project(j - 1)
    down_project(FFN_TILES - 1)
    out_ref[...] = x_ref[...] + acc_scr[...]


def _ffn(x2d, g, wu, dw, wd, seq):
    n = x2d.shape[0]
    tm = ROW_TILE
    nsb = seq // tm
    hb = tm // FFN_HALO
    const = lambda i: (0, 0)
    return pl.pallas_call(
        functools.partial(_ffn_kernel, nsb=nsb),
        grid=(n // tm,),
        in_specs=[
            pl.BlockSpec((tm, D_MODEL), lambda i: (i, 0)),
            pl.BlockSpec((FFN_HALO, D_MODEL), lambda i: (jnp.maximum(i * hb - 1, 0), 0)),
            pl.BlockSpec((1, D_MODEL), lambda i: (0, 0)),
            pl.BlockSpec((D_MODEL, 2 * FFN_HIDDEN), const),
            pl.BlockSpec((FFN_KERNEL, 2 * FFN_HIDDEN), const),
            pl.BlockSpec((FFN_HIDDEN, D_MODEL), const),
        ],
        out_specs=pl.BlockSpec((tm, D_MODEL), lambda i: (i, 0)),
        out_shape=jax.ShapeDtypeStruct((n, D_MODEL), F32),
        scratch_shapes=[
            pltpu.VMEM((tm + FFN_HALO, D_MODEL), BF16),
            pltpu.VMEM((tm, D_MODEL), F32),
        ] + [pltpu.VMEM((tm + FFN_HALO, FFN_TILE), F32)] * 4 + [
            pltpu.VMEM((tm, FFN_TILE), BF16)] * 2,
        compiler_params=_params(("parallel",)),
        name="ffn",
    )(x2d, x2d, g, wu, dw, wd)


def kernel(x, norm_mix, w_in, qk_gain_q, qk_gain_k, lambda_q1, lambda_k1, lambda_q2, lambda_k2, diff_subln, rel_bias, w_attn_out, s5_lambda_re, s5_lambda_im, s5_log_step, s5_b_re, s5_b_im, s5_c_re, s5_c_im, s5_d, s5_glu_w1, s5_glu_w2, conv_dw_w, conv_dw_b, conv_ln_g, conv_ln_b, conv_w_out, w_out, norm_ffn, ffn_w_up, ffn_dw_w, ffn_w_down):
    batch, seq, _ = x.shape
    assert batch == SUBLANES, "the S5 scan keeps one batch entry per sublane"
    n = batch * seq
    row = lambda p: p.reshape(1, -1).astype(F32)
    bias = _bias_tiles(rel_bias.astype(F32))
    x2d = x.reshape(n, D_MODEL)
    for l in range(DEPTH):
        lam_init = 0.8 - 0.6 * math.exp(-0.3 * l)
        q, k, v, u_tm, c_glu, gates = _in_proj(
            x2d.reshape(batch, seq, D_MODEL), row(norm_mix[l]), w_in[l].astype(BF16),
            row(jnp.tile(qk_gain_q[l], 2)), row(jnp.tile(qk_gain_k[l], 2)))

        lamv = jnp.stack([lambda_q1[l], lambda_k1[l], lambda_q2[l], lambda_k2[l]]).astype(F32)
        o = _attention(q, k, v, bias, lamv, row(diff_subln[l]), lam_init)

        a, b_blk, c_blk = _s5_discretise(s5_lambda_re[l], s5_lambda_im[l], s5_log_step[l],
                                         s5_b_re[l], s5_b_im[l], s5_c_re[l], s5_c_im[l])
        y = _s5(u_tm.reshape(seq * batch, D_MODEL), a, b_blk, c_blk, row(s5_d[l]), batch, seq)

        c = _conv_module(c_glu.reshape(n, D_MODEL), conv_dw_w[l].astype(F32), row(conv_dw_b[l]),
                         row(conv_ln_g[l]), row(conv_ln_b[l]), seq)

        x2d = _merge(o.reshape(n, D_MODEL), y.reshape(n, D_MODEL), c,
                     gates.reshape(n, 3 * D_MODEL), x2d,
                     w_attn_out[l].astype(BF16), s5_glu_w1[l].astype(BF16),
                     s5_glu_w2[l].astype(BF16), conv_w_out[l].astype(BF16), w_out[l].astype(BF16))

        x2d = _ffn(x2d, row(norm_ffn[l]), ffn_w_up[l].astype(BF16), ffn_dw_w[l].astype(F32),
                   ffn_w_down[l].astype(BF16), seq)
    return x2d.reshape(batch, seq, D_MODEL)
```
